```python
import math
import jax, jax.numpy as jnp
from jax import lax
import numpy as np

D_MODEL = 2048
BATCH = 4
SEQ = 2048
DEPTH = 4
DEC_BATCH = 128
DEC_SEQ = 4
PAST_LEN = 16384
PAGE_SIZE = 128

N_MIXERS = 2
N_HG = (DEPTH + 1) // 2
N_GLA = DEPTH // 2
HG_KEY = 128
HG_HEADS = D_MODEL // HG_KEY
HG_VAL = D_MODEL // HG_HEADS
HG_DIM = HG_HEADS * HG_KEY
GLA_HEADS = 4
GLA_KEY_DIM = D_MODEL // 2
GLA_VAL_DIM = D_MODEL
GLA_KEY = GLA_KEY_DIM // GLA_HEADS
GLA_VAL = GLA_VAL_DIM // GLA_HEADS
GLA_RANK = 16
GLA_GATE_NORM = 16.0
GLA_IN = 2 * GLA_KEY_DIM + 2 * GLA_VAL_DIM + GLA_RANK
D_FF = 4 * D_MODEL
CHUNK = 64
ALPHA = (2.0 * DEPTH) ** 0.25
BETA = (8.0 * DEPTH) ** -0.25
EPS = 1e-5

kernel_name = "hgrn2_gla_adaln_deepnorm_step"


def layer_norm(x, w, b):
    xf = x.astype(jnp.float32)
    mu = jnp.mean(xf, -1, keepdims=True)
    xc = xf - mu
    var = jnp.mean(xc * xc, -1, keepdims=True)
    return (xc * lax.rsqrt(var + EPS) * w.astype(jnp.float32) + b.astype(jnp.float32)).astype(x.dtype)


def head_rms_norm(o, w):
    of = o.astype(jnp.float32)
    of = of * lax.rsqrt(jnp.mean(of * of, -1, keepdims=True) + EPS)
    return (of * w.astype(jnp.float32)).astype(o.dtype)


def gated_linear_recurrence(q, k, v, log_f, s0):
    B, T, H, K = q.shape
    V = v.shape[-1]
    C = CHUNK if T % CHUNK == 0 else T
    N = T // C
    f32 = jnp.float32

    def to_chunks(a):
        return a.astype(f32).reshape(B, N, C, H, a.shape[-1]).transpose(1, 0, 3, 2, 4)

    qc, kc, vc, gc = to_chunks(q), to_chunks(k), to_chunks(v), to_chunks(log_f)
    causal = jnp.tril(jnp.ones((C, C), bool))[None, None, :, :, None]

    def step(S, inp):
        qb, kb, vb, gb = inp
        b = jnp.cumsum(gb, axis=2)
        diff = b[:, :, :, None, :] - b[:, :, None, :, :]
        decay = jnp.exp(jnp.where(causal, diff, -jnp.inf))
        att = jnp.einsum('bhtk,bhsk,bhtsk->bhts', qb, kb, decay)
        o = (jnp.einsum('bhts,bhsv->bhtv', att, vb)
             + jnp.einsum('bhtk,bhkv->bhtv', qb * jnp.exp(b), S))
        b_last = b[:, :, -1:, :]
        S_new = (jnp.exp(b_last[:, :, 0, :])[..., None] * S
                 + jnp.einsum('bhsk,bhsv->bhkv', kb * jnp.exp(b_last - b), vb))
        return S_new, o

    S, o = lax.scan(step, s0.astype(f32), (qc, kc, vc, gc))
    o = o.transpose(1, 0, 3, 2, 4).reshape(B, T, H, V)
    return o.astype(q.dtype), S.astype(s0.dtype)


def hgrn2_mixer(h, w_in, lb, norm_w, w_out, s0):
    B, T, _ = h.shape
    q, f_pre, i, g = jnp.split(h @ w_in, 4, axis=-1)
    fp = f_pre.astype(jnp.float32)
    log_f = jnp.logaddexp(jnp.log(lb), jnp.log1p(-lb) + jax.nn.log_sigmoid(fp))
    k = (1.0 - lb) * jax.nn.sigmoid(-fp)
    hd = lambda a, d: a.reshape(B, T, HG_HEADS, d)
    o, s = gated_linear_recurrence(hd(jax.nn.silu(q), HG_KEY), hd(k.astype(h.dtype), HG_KEY),
                                   hd(i, HG_VAL), hd(log_f, HG_KEY), s0)
    o = head_rms_norm(o, norm_w.reshape(HG_HEADS, HG_VAL)).reshape(B, T, HG_DIM) * jax.nn.silu(g)
    return o @ w_out, s


def gla_mixer(h, w_in, w_gk2, b_gk, norm_w, w_out, s0):
    B, T, _ = h.shape
    q, k, v, g, gk_low = jnp.split(
        h @ w_in, np.cumsum([GLA_KEY_DIM, GLA_KEY_DIM, GLA_VAL_DIM, GLA_VAL_DIM]).tolist(), axis=-1)
    log_f = jax.nn.log_sigmoid((gk_low @ w_gk2 + b_gk).astype(jnp.float32)) / GLA_GATE_NORM
    hd = lambda a, d: a.reshape(B, T, GLA_HEADS, d)
    o, s = gated_linear_recurrence(hd(q * GLA_KEY ** -0.5, GLA_KEY), hd(k, GLA_KEY),
                                   hd(v, GLA_VAL), hd(log_f, GLA_KEY), s0)
    o = head_rms_norm(o, norm_w.reshape(GLA_HEADS, GLA_VAL)).reshape(B, T, GLA_VAL_DIM) * jax.nn.silu(g)
    return o @ w_out, s


def trunk(x, c, s_hg, s_gla, w_ada, b_ada, ln_w, ln_b, hg_w_in, hg_lb, hg_norm_w, hg_w_out,
          gla_w_in, gla_w_gk2, gla_b_gk, gla_norm_w, gla_w_out, w_up, w_down):
    new_hg, new_gla = [], []
    cs = jax.nn.silu(c)
    for l in range(DEPTH):
        mod = (cs @ w_ada[l] + b_ada[l])[:, None, :]
        sh1, sc1, g1, sh2, sc2, g2 = jnp.split(mod, 6, axis=-1)
        h = x * (1.0 + sc1) + sh1
        j = l // N_MIXERS
        if l % N_MIXERS == 0:
            y, s = hgrn2_mixer(h, hg_w_in[j], hg_lb[j], hg_norm_w[j], hg_w_out[j], s_hg[j])
            new_hg.append(s)
        else:
            y, s = gla_mixer(h, gla_w_in[j], gla_w_gk2[j], gla_b_gk[j], gla_norm_w[j], gla_w_out[j], s_gla[j])
            new_gla.append(s)
        x = layer_norm(ALPHA * x + (1.0 + g1) * y, ln_w[l, 0], ln_b[l, 0])
        h = x * (1.0 + sc2) + sh2
        y = jnp.square(jax.nn.relu(h @ w_up[l])) @ w_down[l]
        x = layer_norm(ALPHA * x + (1.0 + g2) * y, ln_w[l, 1], ln_b[l, 1])
    return x, jnp.stack(new_hg), jnp.stack(new_gla)


def setup_inputs(seed: int = 0) -> dict:
    key = jax.random.key(seed)
    ks = jax.random.split(key, 32)
    nrm = lambda k, shape, s: jax.random.normal(k, shape, jnp.float32) * s
    d_sc = D_MODEL ** -0.5
    hg_w_in = jnp.concatenate([
        nrm(ks[0], (N_HG, D_MODEL, HG_DIM), d_sc),
        nrm(ks[1], (N_HG, D_MODEL, HG_DIM), d_sc),
        nrm(ks[2], (N_HG, D_MODEL, HG_DIM), d_sc * BETA),
        nrm(ks[3], (N_HG, D_MODEL, HG_DIM), d_sc)], -1)
    gla_w_in = jnp.concatenate([
        nrm(ks[4], (N_GLA, D_MODEL, GLA_KEY_DIM), d_sc),
        nrm(ks[5], (N_GLA, D_MODEL, GLA_KEY_DIM), d_sc),
        nrm(ks[6], (N_GLA, D_MODEL, GLA_VAL_DIM), d_sc * BETA),
        nrm(ks[7], (N_GLA, D_MODEL, GLA_VAL_DIM), d_sc),
        nrm(ks[8], (N_GLA, D_MODEL, GLA_RANK), d_sc)], -1)
    return {
        "x_prompt": nrm(ks[9], (BATCH, SEQ, D_MODEL), 1.0),
        "x_sample": nrm(ks[10], (DEC_BATCH, DEC_SEQ, D_MODEL), 1.0),
        "state_hgrn": nrm(ks[11], (N_HG, DEC_BATCH, HG_HEADS, HG_KEY, HG_VAL), 0.5),
        "state_gla": nrm(ks[12], (N_GLA, DEC_BATCH, GLA_HEADS, GLA_KEY, GLA_VAL), 0.5),
        "c_prompt": nrm(ks[13], (BATCH, D_MODEL), 1.0),
        "c_sample": nrm(ks[14], (DEC_BATCH, D_MODEL), 1.0),
        "w_ada": nrm(ks[15], (DEPTH, D_MODEL, 6 * D_MODEL), d_sc * 0.5),
        "b_ada": nrm(ks[16], (DEPTH, 6 * D_MODEL), 0.02),
        "ln_w": 1.0 + nrm(ks[17], (DEPTH, 2, D_MODEL), 0.02),
        "ln_b": nrm(ks[18], (DEPTH, 2, D_MODEL), 0.02),
        "hg_w_in": hg_w_in,
        "hg_lb_raw": nrm(ks[19], (N_HG, HG_DIM), 0.5),
        "hg_norm_w": 1.0 + nrm(ks[20], (N_HG, HG_DIM), 0.02),
        "hg_w_out": nrm(ks[21], (N_HG, HG_DIM, D_MODEL), HG_DIM ** -0.5 * BETA),
        "gla_w_in": gla_w_in,
        "gla_w_gk2": nrm(ks[22], (N_GLA, GLA_RANK, GLA_KEY_DIM), GLA_RANK ** -0.5),
        "gla_b_gk": nrm(ks[23], (N_GLA, GLA_KEY_DIM), 0.1),
        "gla_norm_w": 1.0 + nrm(ks[24], (N_GLA, GLA_VAL_DIM), 0.02),
        "gla_w_out": nrm(ks[25], (N_GLA, GLA_VAL_DIM, D_MODEL), GLA_VAL_DIM ** -0.5 * BETA),
        "w_up": nrm(ks[26], (DEPTH, D_MODEL, D_FF), d_sc),
        "w_down": nrm(ks[27], (DEPTH, D_FF, D_MODEL), D_FF ** -0.5 * BETA),
    }


def reference(x_prompt, x_sample, state_hgrn, state_gla, c_prompt, c_sample, w_ada, b_ada, ln_w, ln_b,
              hg_w_in, hg_lb_raw, hg_norm_w, hg_w_out, gla_w_in, gla_w_gk2, gla_b_gk, gla_norm_w,
              gla_w_out, w_up, w_down):
    p = jax.nn.softmax(hg_lb_raw.astype(jnp.float32), axis=0)
    hg_lb = jnp.clip(jnp.cumsum(p, axis=0) - p[0], 0.0, 1.0 - 1e-6)
    weights = (w_ada, b_ada, ln_w, ln_b, hg_w_in, hg_lb, hg_norm_w, hg_w_out,
               gla_w_in, gla_w_gk2, gla_b_gk, gla_norm_w, gla_w_out, w_up, w_down)
    zero_hg = jnp.zeros((N_HG, x_prompt.shape[0], HG_HEADS, HG_KEY, HG_VAL), state_hgrn.dtype)
    zero_gla = jnp.zeros((N_GLA, x_prompt.shape[0], GLA_HEADS, GLA_KEY, GLA_VAL), state_gla.dtype)
    y_prompt, new_hgrn_prompt, new_gla_prompt = trunk(x_prompt, c_prompt, zero_hg, zero_gla, *weights)
    y_sample, new_hgrn_sample, new_gla_sample = trunk(x_sample, c_sample, state_hgrn, state_gla, *weights)
    return (y_prompt, y_sample, new_hgrn_prompt, new_gla_prompt, new_hgrn_sample, new_gla_sample)
```

```python
import functools

import jax
import jax.numpy as jnp
from jax import lax
from jax.experimental import pallas as pl
from jax.experimental.pallas import tpu as pltpu

F32 = jnp.float32
BF16 = jnp.bfloat16

EPS = 1e-5
GLA_GATE_NORM = 16.0
LANES = 128
CHUNK = 16
VMEM_LIMIT = 56 * 1024 * 1024


def _sigmoid(x):
    return 1.0 / (1.0 + jnp.exp(-x))


def _silu(x):
    return x * _sigmoid(x)


def _log_sigmoid(x):
    return jnp.minimum(x, 0.0) - jnp.log1p(jnp.exp(-jnp.abs(x)))


def _layer_norm(z, w, b):
    mu = jnp.mean(z, axis=-1, keepdims=True)
    zc = z - mu
    var = jnp.mean(zc * zc, axis=-1, keepdims=True)
    return zc * lax.rsqrt(var + EPS) * w + b


def _params(sem, vmem=None):
    return pltpu.CompilerParams(dimension_semantics=sem, vmem_limit_bytes=vmem)


def _ada_kernel(c_ref, w_ref, b_ref, o_ref):
    cs = _silu(c_ref[...]).astype(BF16)
    o_ref[...] = jnp.dot(cs, w_ref[...].astype(BF16), preferred_element_type=F32) + b_ref[...]


def _ada(c_all, w_ada, b_ada):
    depth, d, n6 = w_ada.shape
    rows = c_all.shape[0]
    tn = 1024
    return pl.pallas_call(
        _ada_kernel,
        grid=(depth, n6 // tn),
        in_specs=[pl.BlockSpec((rows, d), lambda l, j: (0, 0)),
                  pl.BlockSpec((None, d, tn), lambda l, j: (l, 0, j)),
                  pl.BlockSpec((None, 1, tn), lambda l, j: (l, 0, j))],
        out_specs=pl.BlockSpec((None, rows, tn), lambda l, j: (l, 0, j)),
        out_shape=jax.ShapeDtypeStruct((depth, rows, n6), F32),
        compiler_params=_params(("arbitrary", "arbitrary"), VMEM_LIMIT),
        name="ada_proj",
    )(c_all, w_ada, b_ada.reshape(depth, 1, n6))


class _Group:
    def __init__(self, batch, seq, d, tm, mods, per_token):
        self.batch, self.seq, self.d, self.tm = batch, seq, d, tm
        self.m = batch * seq
        self.mods = mods
        self.per_token = per_token

    def mod_spec(self, chunk):
        d, tm = self.d, self.tm
        if self.per_token:
            return pl.BlockSpec((None, tm, d), lambda i, *_: (chunk, i, 0))
        tiles_per_b = self.seq // tm
        return pl.BlockSpec((None, 1, d), lambda i, *_: ((i // tiles_per_b) * 6 + chunk, 0, 0))


def _mod0_kernel(x_ref, sc_ref, sh_ref, h_ref):
    h_ref[...] = (x_ref[...] * (1.0 + sc_ref[...]) + sh_ref[...]).astype(BF16)


def _mod0(grp, x, layer):
    m, d, tm = grp.m, grp.d, grp.tm
    mod = grp.mods[layer]
    return pl.pallas_call(
        _mod0_kernel,
        grid=(m // tm,),
        in_specs=[pl.BlockSpec((tm, d), lambda i: (i, 0)), grp.mod_spec(1), grp.mod_spec(0)],
        out_specs=pl.BlockSpec((tm, d), lambda i: (i, 0)),
        out_shape=jax.ShapeDtypeStruct((m, d), BF16),
        compiler_params=_params(("arbitrary",)),
        name="mod0",
    )(x, mod, mod)


def _mm_kernel(h_ref, w_ref, o_ref):
    o_ref[...] = jnp.dot(h_ref[...], w_ref[...], preferred_element_type=F32)


def _inproj(grp, h, w):
    m, d, tm = grp.m, grp.d, grp.tm
    n = w.shape[1]
    tn = 1024
    return pl.pallas_call(
        _mm_kernel,
        grid=(n // tn, m // tm),
        in_specs=[pl.BlockSpec((tm, d), lambda j, i: (i, 0)),
                  pl.BlockSpec((d, tn), lambda j, i: (0, j))],
        out_specs=pl.BlockSpec((tm, tn), lambda j, i: (i, j)),
        out_shape=jax.ShapeDtypeStruct((m, n), F32),
        compiler_params=_params(("arbitrary", "arbitrary"), VMEM_LIMIT),
        name="in_proj",
    )(h, w)


def _gate_kernel(h_ref, wl_ref, w2_ref, b_ref, o_ref):
    low = jnp.dot(h_ref[...], wl_ref[...], preferred_element_type=F32)
    gk = jnp.dot(low.astype(BF16), w2_ref[...], preferred_element_type=F32) + b_ref[...]
    o_ref[...] = _log_sigmoid(gk) * (1.0 / GLA_GATE_NORM)


def _gla_gate(grp, h, w_low, w_gk2, b_gk):
    m, d, tm = grp.m, grp.d, grp.tm
    kd = w_gk2.shape[1]
    return pl.pallas_call(
        _gate_kernel,
        grid=(m // tm,),
        in_specs=[pl.BlockSpec((tm, d), lambda i: (i, 0)),
                  pl.BlockSpec((d, LANES), lambda i: (0, 0)),
                  pl.BlockSpec((LANES, kd), lambda i: (0, 0)),
                  pl.BlockSpec((1, kd), lambda i: (0, 0))],
        out_specs=pl.BlockSpec((tm, kd), lambda i: (i, 0)),
        out_shape=jax.ShapeDtypeStruct((m, kd), F32),
        compiler_params=_params(("arbitrary",)),
        name="gla_gate",
    )(h, w_low, w_gk2, b_gk.reshape(1, kd))


def _chunk_core(q, k, v, lf, s, valid):
    c, kdim = q.shape
    vdim = v.shape[1]
    row = lax.broadcasted_iota(jnp.int32, (c, kdim), 0)
    b = jnp.broadcast_to(lf[0:1, :], (c, kdim))
    for j in range(1, valid):
        b = b + jnp.where(row >= j, lf[j:j + 1, :], 0.0)
    b_last = b[c - 1:c, :]
    qd = (q * jnp.exp(b)).astype(BF16)
    kd = (k * jnp.exp(b_last - b)).astype(BF16)
    o = jnp.dot(qd, s.astype(BF16), preferred_element_type=F32)
    rowc = lax.broadcasted_iota(jnp.int32, (c, 1), 0)
    for j in range(valid):
        w = q * k[j:j + 1, :] * jnp.exp(jnp.minimum(b - b[j:j + 1, :], 0.0))
        a = jnp.sum(w, axis=1, keepdims=True)
        a = jnp.where(rowc >= j, a, 0.0)
        o = o + a * v[j:j + 1, :]
    e_col = jnp.transpose(jnp.broadcast_to(jnp.exp(b_last), (LANES, kdim)))
    if vdim > LANES:
        e_col = jnp.concatenate([e_col] * (vdim // LANES), axis=1)
    upd = lax.dot_general(kd, v.astype(BF16), (((0,), (0,)), ((), ())), preferred_element_type=F32)
    return o, s * e_col + upd


def _gated_out(o, g, nw):
    ms = jnp.mean(o * o, axis=-1, keepdims=True)
    return (o * lax.rsqrt(ms + EPS) * nw * _silu(g)).astype(BF16)


def _rec_kernel(*refs, mixer, layer_j, has_s0, bb, tb, valid, n_tblocks):
    refs = list(refs)
    if mixer == "hgrn":
        a_ref, f_ref, v_ref, g_ref, lb_ref, nw_ref = refs[:6]
        refs = refs[6:]
    else:
        a_ref, k_ref, v_ref, g_ref, f_ref, nw_ref = refs[:6]
        refs = refs[6:]
    s0_ref = refs.pop(0) if has_s0 else None
    og_ref, so_ref, s_scr = refs[:3]
    pad_refs = refs[3:]
    kdim = a_ref.shape[-1]
    vdim = v_ref.shape[-1]
    ti = pl.program_id(2)

    @pl.when(ti == 0)
    def _():
        if has_s0:
            s_scr[...] = s0_ref[...]
        else:
            s_scr[...] = jnp.zeros(s_scr.shape, F32)

    if pad_refs:
        for r in pad_refs:
            r[...] = jnp.zeros(r.shape, F32)

    if mixer == "hgrn":
        raw = lb_ref[...]
        ex = jnp.exp(raw - jnp.max(raw, axis=0, keepdims=True))
        p = ex / jnp.sum(ex, axis=0, keepdims=True)
        cum = p[0:1, :]
        for r in range(1, layer_j + 1):
            cum = cum + p[r:r + 1, :]
        lb = jnp.clip(cum - p[0:1, :], 0.0, 1.0 - 1e-6)
        log_lb = jnp.log(lb)
        log_1m = jnp.log1p(-lb)
        one_m = 1.0 - lb
    nw = nw_ref[...]
    n_chunks = tb // valid

    def per_chunk(bi, ci):
        rows = pl.ds(0, valid) if n_chunks == 1 else pl.ds(pl.multiple_of(ci * valid, valid), valid)
        a = a_ref[bi, rows, :]
        v = v_ref[bi, rows, :]
        if mixer == "hgrn":
            fp = f_ref[bi, rows, :]
            q = _silu(a)
            y = log_1m + _log_sigmoid(fp)
            lf = jnp.maximum(log_lb, y) + jnp.log1p(jnp.exp(-jnp.abs(log_lb - y)))
            k = one_m * _sigmoid(-fp)
        else:
            q = a * (kdim ** -0.5)
            k = k_ref[bi, rows, :]
            lf = f_ref[bi, rows, :]
        if pad_refs:
            pq, pk, pf, pv = pad_refs
            pq[0:valid, :] = q
            pk[0:valid, :] = k
            pf[0:valid, :] = lf
            pv[0:valid, :] = v
            q, k, lf, v = pq[...], pk[...], pf[...], pv[...]
        o, s_new = _chunk_core(q, k, v, lf, s_scr[bi], valid)
        s_scr[bi] = s_new
        og_ref[bi, rows, :] = _gated_out(o[0:valid, :], g_ref[bi, rows, :], nw)

    def per_batch(bi, carry):
        if n_chunks == 1:
            per_chunk(bi, 0)
        else:
            lax.fori_loop(0, n_chunks, lambda ci, c: (per_chunk(bi, ci), c)[1], 0)
        return carry

    if bb == 1:
        per_batch(0, 0)
    else:
        lax.fori_loop(0, bb, per_batch, 0)

    @pl.when(ti == n_tblocks - 1)
    def _():
        so_ref[...] = s_scr[...]


def _recurrence(mixer, layer_j, proj3, lf3, lb_raw, norm_w, state, heads, kdim, vdim, bb, tb):
    bsz, t, _ = proj3.shape
    has_s0 = state is not None
    n_tblocks = t // tb
    valid = min(CHUNK, tb)
    kb, vb = kdim, vdim
    nk = heads * kdim // kb
    nv = heads * vdim // vb

    def tok(width, base):
        return pl.BlockSpec((bb, tb, width), lambda b, h, ti: (b, ti, base + h))

    if mixer == "hgrn":
        in_specs = [tok(kb, 0), tok(kb, nk), tok(vb, 2 * nk), tok(vb, 3 * nk),
                    pl.BlockSpec((lb_raw.shape[0], kb), lambda b, h, ti: (0, h)),
                    pl.BlockSpec((1, vb), lambda b, h, ti: (0, h))]
        args = [proj3, proj3, proj3, proj3, lb_raw, norm_w.reshape(1, -1)]
    else:
        v_base = 2 * heads * kdim // vb
        in_specs = [tok(kb, 0), tok(kb, nk), tok(vb, v_base), tok(vb, v_base + nv), tok(kb, 0),
                    pl.BlockSpec((1, vb), lambda b, h, ti: (0, h))]
        args = [proj3, proj3, proj3, proj3, lf3, norm_w.reshape(1, -1)]
    if has_s0:
        in_specs.append(pl.BlockSpec((None, bb, None, kdim, vdim), lambda b, h, ti: (layer_j, b, h, 0, 0)))
        args.append(state)
    scratch = [pltpu.VMEM((bb, kdim, vdim), F32)]
    if valid < CHUNK:
        scratch += [pltpu.VMEM((CHUNK, kdim), F32)] * 3 + [pltpu.VMEM((CHUNK, vdim), F32)]
    kern = functools.partial(_rec_kernel, mixer=mixer, layer_j=layer_j, has_s0=has_s0, bb=bb, tb=tb,
                             valid=valid, n_tblocks=n_tblocks)
    return pl.pallas_call(
        kern,
        grid=(bsz // bb, heads, n_tblocks),
        in_specs=in_specs,
        out_specs=[pl.BlockSpec((bb, tb, vb), lambda b, h, ti: (b, ti, h)),
                   pl.BlockSpec((bb, None, kdim, vdim), lambda b, h, ti: (b, h, 0, 0))],
        out_shape=[jax.ShapeDtypeStruct((bsz, t, heads * vdim), BF16),
                   jax.ShapeDtypeStruct((bsz, heads, kdim, vdim), F32)],
        scratch_shapes=scratch,
        compiler_params=_params(("arbitrary", "arbitrary", "arbitrary"), VMEM_LIMIT),
        name=mixer + "_rec",
    )(*args)


def _outproj_kernel(og_ref, w_ref, x_ref, g_ref, sc_ref, sh_ref, lnw_ref, lnb_ref, xo_ref, ho_ref, *, alpha):
    y = jnp.dot(og_ref[...], w_ref[...], preferred_element_type=F32)
    xn = _layer_norm(alpha * x_ref[...] + (1.0 + g_ref[...]) * y, lnw_ref[...], lnb_ref[...])
    xo_ref[...] = xn
    ho_ref[...] = (xn * (1.0 + sc_ref[...]) + sh_ref[...]).astype(BF16)


def _outproj(grp, og, w_out, x, layer, ln_w, ln_b, alpha):
    m, d = grp.m, grp.d
    tm = min(grp.tm, 256)
    sub = _Group(grp.batch, grp.seq, d, tm, grp.mods, grp.per_token)
    din = w_out.shape[0]
    mod = grp.mods[layer]
    row = lambda i: (i, 0)
    fixed = lambda i: (0, 0)
    return pl.pallas_call(
        functools.partial(_outproj_kernel, alpha=alpha),
        grid=(m // tm,),
        in_specs=[pl.BlockSpec((tm, din), row), pl.BlockSpec((din, d), fixed), pl.BlockSpec((tm, d), row),
                  sub.mod_spec(2), sub.mod_spec(4), sub.mod_spec(3),
                  pl.BlockSpec((1, d), fixed), pl.BlockSpec((1, d), fixed)],
        out_specs=[pl.BlockSpec((tm, d), row), pl.BlockSpec((tm, d), row)],
        out_shape=[jax.ShapeDtypeStruct((m, d), F32), jax.ShapeDtypeStruct((m, d), BF16)],
        compiler_params=_params(("arbitrary",), VMEM_LIMIT),
        name="out_proj_ln",
    )(og, w_out, x, mod, mod, mod, ln_w.reshape(1, d), ln_b.reshape(1, d))


def _mlp_kernel(*refs, alpha, n_f, emit_h):
    if emit_h:
        h_ref, wu_ref, wd_ref, x_ref, g_ref, sc_ref, sh_ref, lnw_ref, lnb_ref, xo_ref, ho_ref, acc_ref = refs
    else:
        h_ref, wu_ref, wd_ref, x_ref, g_ref, lnw_ref, lnb_ref, xo_ref, acc_ref = refs
    f = pl.program_id(1)
    u = jnp.dot(h_ref[...], wu_ref[...], preferred_element_type=F32)
    u = jnp.square(jnp.maximum(u, 0.0)).astype(BF16)
    part = jnp.dot(u, wd_ref[...], preferred_element_type=F32)

    @pl.when(f == 0)
    def _():
        acc_ref[...] = part

    @pl.when(f > 0)
    def _():
        acc_ref[...] += part

    @pl.when(f == n_f - 1)
    def _():
        xn = _layer_norm(alpha * x_ref[...] + (1.0 + g_ref[...]) * acc_ref[...], lnw_ref[...], lnb_ref[...])
        xo_ref[...] = xn
        if emit_h:
            ho_ref[...] = (xn * (1.0 + sc_ref[...]) + sh_ref[...]).astype(BF16)


def _mlp(grp, h, w_up, w_down, x, layer, ln_w, ln_b, alpha, emit_h):
    m, d, tm = grp.m, grp.d, grp.tm
    dff = w_up.shape[1]
    tf = 512
    n_f = dff // tf
    row = lambda i, f: (i, 0)
    fixed = lambda i, f: (0, 0)
    in_specs = [pl.BlockSpec((tm, d), row), pl.BlockSpec((d, tf), lambda i, f: (0, f)),
                pl.BlockSpec((tf, d), lambda i, f: (f, 0)), pl.BlockSpec((tm, d), row), grp.mod_spec(5)]
    args = [h, w_up, w_down, x, grp.mods[layer]]
    out_specs = [pl.BlockSpec((tm, d), row)]
    out_shape = [jax.ShapeDtypeStruct((m, d), F32)]
    if emit_h:
        in_specs += [grp.mod_spec(1), grp.mod_spec(0)]
        args += [grp.mods[layer + 1], grp.mods[layer + 1]]
        out_specs.append(pl.BlockSpec((tm, d), row))
        out_shape.append(jax.ShapeDtypeStruct((m, d), BF16))
    in_specs += [pl.BlockSpec((1, d), fixed), pl.BlockSpec((1, d), fixed)]
    args += [ln_w.reshape(1, d), ln_b.reshape(1, d)]
    outs = pl.pallas_call(
        functools.partial(_mlp_kernel, alpha=alpha, n_f=n_f, emit_h=emit_h),
        grid=(m // tm, n_f),
        in_specs=in_specs,
        out_specs=out_specs,
        out_shape=out_shape,
        scratch_shapes=[pltpu.VMEM((tm, d), F32)],
        compiler_params=_params(("arbitrary", "arbitrary"), VMEM_LIMIT),
        name="mlp_ln",
    )(*args)
    return (outs[0], outs[1]) if emit_h else (outs[0], None)


def _trunk(grp, x, s_hg, s_gla, w, rec_cfg):
    depth = w["w_up"].shape[0]
    alpha = (2.0 * depth) ** 0.25
    d = grp.d
    hg_heads, hg_k, hg_v = w["hg_dims"]
    gla_heads, gla_k, gla_v = w["gla_dims"]
    new_hg, new_gla = [], []
    h = _mod0(grp, x, 0)
    for l in range(depth):
        j = l // 2
        if l % 2 == 0:
            proj = _inproj(grp, h, w["hg_w_in"][j])
            og, s = _recurrence("hgrn", j, proj.reshape(grp.batch, grp.seq, -1), None, w["hg_lb_raw"],
                                w["hg_norm_w"][j], s_hg, hg_heads, hg_k, hg_v, rec_cfg["hg_bb"], rec_cfg["tb"])
            new_hg.append(s)
            w_out = w["hg_w_out"][j]
        else:
            proj = _inproj(grp, h, w["gla_w_main"][j])
            lf = _gla_gate(grp, h, w["gla_w_low"][j], w["gla_w_gk2"][j], w["gla_b_gk"][j])
            og, s = _recurrence("gla", j, proj.reshape(grp.batch, grp.seq, -1),
                                lf.reshape(grp.batch, grp.seq, -1), None, w["gla_norm_w"][j], s_gla,
                                gla_heads, gla_k, gla_v, rec_cfg["gla_bb"], rec_cfg["tb"])
            new_gla.append(s)
            w_out = w["gla_w_out"][j]
        x, h2 = _outproj(grp, og.reshape(grp.m, -1), w_out, x, l, w["ln_w"][l, 0], w["ln_b"][l, 0], alpha)
        x, h = _mlp(grp, h2, w["w_up"][l], w["w_down"][l], x, l, w["ln_w"][l, 1], w["ln_b"][l, 1], alpha,
                    emit_h=(l + 1 < depth))
    return x, jnp.stack(new_hg), jnp.stack(new_gla)


def kernel(x_prompt, x_sample, state_hgrn, state_gla, c_prompt, c_sample, w_ada, b_ada, ln_w, ln_b,
           hg_w_in, hg_lb_raw, hg_norm_w, hg_w_out, gla_w_in, gla_w_gk2, gla_b_gk, gla_norm_w,
           gla_w_out, w_up, w_down):
    bp, tp, d = x_prompt.shape
    bs, ts, _ = x_sample.shape
    depth = w_up.shape[0]
    _, _, hg_heads, hg_k, hg_v = state_hgrn.shape
    _, _, gla_heads, gla_k, gla_v = state_gla.shape
    gla_kdim = gla_heads * gla_k
    gla_main = 2 * gla_kdim + 2 * gla_heads * gla_v
    rank = gla_w_in.shape[-1] - gla_main

    pad_rows = (-(bs + bp)) % 8
    c_all = jnp.concatenate([c_sample, c_prompt, jnp.zeros((pad_rows, d), F32)], axis=0)
    mod = _ada(c_all, w_ada, b_ada)
    mod_s = mod[:, :bs].reshape(depth, bs, 6, d).transpose(0, 2, 1, 3)
    mod_s = jnp.repeat(mod_s, ts, axis=2)
    mod_p = mod[:, bs:bs + bp].reshape(depth, bp * 6, 1, d)

    w = {
        "hg_w_in": hg_w_in.astype(BF16), "hg_lb_raw": hg_lb_raw, "hg_norm_w": hg_norm_w,
        "hg_w_out": hg_w_out.astype(BF16),
        "gla_w_main": gla_w_in[..., :gla_main].astype(BF16),
        "gla_w_low": jnp.pad(gla_w_in[..., gla_main:], ((0, 0), (0, 0), (0, LANES - rank))).astype(BF16),
        "gla_w_gk2": jnp.pad(gla_w_gk2, ((0, 0), (0, LANES - rank), (0, 0))).astype(BF16),
        "gla_b_gk": gla_b_gk, "gla_norm_w": gla_norm_w, "gla_w_out": gla_w_out.astype(BF16),
        "w_up": w_up.astype(BF16), "w_down": w_down.astype(BF16), "ln_w": ln_w, "ln_b": ln_b,
        "hg_dims": (hg_heads, hg_k, hg_v), "gla_dims": (gla_heads, gla_k, gla_v),
    }

    grp_p = _Group(bp, tp, d, 512, [mod_p[l] for l in range(depth)], per_token=False)
    grp_s = _Group(bs, ts, d, 512, [mod_s[l] for l in range(depth)], per_token=True)

    y_p, hg_p, gla_p = _trunk(grp_p, x_prompt.reshape(bp * tp, d), None, None, w,
                              {"hg_bb": 1, "gla_bb": 1, "tb": 512})
    y_s, hg_s, gla_s = _trunk(grp_s, x_sample.reshape(bs * ts, d), state_hgrn, state_gla, w,
                              {"hg_bb": 16, "gla_bb": 4, "tb": ts})
    return (y_p.reshape(bp, tp, d), y_s.reshape(bs, ts, d), hg_p, gla_p, hg_s, gla_s)
```

```python
import functools

import jax
import jax.numpy as jnp
from jax import lax
from jax.experimental import pallas as pl
from jax.experimental.pallas import tpu as pltpu

F32 = jnp.float32
BF16 = jnp.bfloat16

EPS = 1e-5
GLA_GATE_NORM = 16.0
LANES = 128
SUBLANES = 8
CHUNK = 16
VMEM_LIMIT = 56 * 1024 * 1024


def _sigmoid(x):
    return 1.0 / (1.0 + jnp.exp(-x))


def _silu(x):
    return x * _sigmoid(x)


def _log_sigmoid(x):
    return jnp.minimum(x, 0.0) - jnp.log1p(jnp.exp(-jnp.abs(x)))


def _layer_norm(z, w, b):
    mu = jnp.mean(z, axis=-1, keepdims=True)
    zc = z - mu
    var = jnp.mean(zc * zc, axis=-1, keepdims=True)
    return zc * lax.rsqrt(var + EPS) * w + b


def _params(sem, vmem=None):
    return pltpu.CompilerParams(dimension_semantics=sem, vmem_limit_bytes=vmem)


def _ada_kernel(c_ref, w_ref, b_ref, o_ref):
    cs = _silu(c_ref[...]).astype(BF16)
    o_ref[...] = jnp.dot(cs, w_ref[...].astype(BF16), preferred_element_type=F32) + b_ref[...]


def _ada(c_all, w_ada, b_ada):
    depth, d, n6 = w_ada.shape
    rows = c_all.shape[0]
    tn = 1024
    return pl.pallas_call(
        _ada_kernel,
        grid=(depth, n6 // tn),
        in_specs=[pl.BlockSpec((rows, d), lambda l, j: (0, 0)),
                  pl.BlockSpec((None, d, tn), lambda l, j: (l, 0, j)),
                  pl.BlockSpec((None, 1, tn), lambda l, j: (l, 0, j))],
        out_specs=pl.BlockSpec((None, rows, tn), lambda l, j: (l, 0, j)),
        out_shape=jax.ShapeDtypeStruct((depth, rows, n6), F32),
        compiler_params=_params(("arbitrary", "arbitrary"), VMEM_LIMIT),
        name="ada_proj",
    )(c_all, w_ada, b_ada.reshape(depth, 1, n6))


class _Group:
    def __init__(self, batch, seq, d, tm, mod, per_token):
        self.batch, self.seq, self.d, self.tm = batch, seq, d, tm
        self.m = batch * seq
        self.mod = mod
        self.per_token = per_token

    def with_tile(self, tm):
        return _Group(self.batch, self.seq, self.d, tm, self.mod, self.per_token)

    def mod_spec(self, layer, chunk):
        d, tm = self.d, self.tm
        if self.per_token:
            return pl.BlockSpec((None, tm, d), lambda i, *_: (layer, i, chunk))
        tiles_per_b = self.seq // tm
        return pl.BlockSpec((None, None, 1, d), lambda i, *_: (layer, i // tiles_per_b, 0, chunk))


def _mod0_kernel(x_ref, sc_ref, sh_ref, h_ref):
    h_ref[...] = (x_ref[...] * (1.0 + sc_ref[...]) + sh_ref[...]).astype(BF16)


def _mod0(grp, x):
    m, d, tm = grp.m, grp.d, grp.tm
    return pl.pallas_call(
        _mod0_kernel,
        grid=(m // tm,),
        in_specs=[pl.BlockSpec((tm, d), lambda i: (i, 0)), grp.mod_spec(0, 1), grp.mod_spec(0, 0)],
        out_specs=pl.BlockSpec((tm, d), lambda i: (i, 0)),
        out_shape=jax.ShapeDtypeStruct((m, d), BF16),
        compiler_params=_params(("arbitrary",)),
        name="mod0",
    )(x, grp.mod, grp.mod)


def _mm_kernel(h_ref, w_ref, o_ref):
    o_ref[...] = jnp.dot(h_ref[...], w_ref[...], preferred_element_type=F32)


def _inproj(grp, h, w, layer, n):
    m, d, tm = grp.m, grp.d, grp.tm
    tn = 1024
    return pl.pallas_call(
        _mm_kernel,
        grid=(n // tn, m // tm),
        in_specs=[pl.BlockSpec((tm, d), lambda j, i: (i, 0)),
                  pl.BlockSpec((None, d, tn), lambda j, i: (layer, 0, j))],
        out_specs=pl.BlockSpec((tm, tn), lambda j, i: (i, j)),
        out_shape=jax.ShapeDtypeStruct((m, n), F32),
        compiler_params=_params(("arbitrary", "arbitrary"), VMEM_LIMIT),
        name="in_proj",
    )(h, w)


def _gate_kernel(h_ref, wl_ref, w2_ref, b_ref, o_ref):
    low = jnp.dot(h_ref[...], wl_ref[...], preferred_element_type=F32)
    gk = jnp.dot(low.astype(BF16), w2_ref[...], preferred_element_type=F32) + b_ref[...]
    o_ref[...] = _log_sigmoid(gk) * (1.0 / GLA_GATE_NORM)


def _gla_gate(grp, h, w_low, w_gk2, b_gk, layer):
    m, d, tm = grp.m, grp.d, grp.tm
    kd = w_gk2.shape[-1]
    return pl.pallas_call(
        _gate_kernel,
        grid=(m // tm,),
        in_specs=[pl.BlockSpec((tm, d), lambda i: (i, 0)),
                  pl.BlockSpec((None, d, LANES), lambda i: (layer, 0, 0)),
                  pl.BlockSpec((None, LANES, kd), lambda i: (layer, 0, 0)),
                  pl.BlockSpec((None, 1, kd), lambda i: (layer, 0, 0))],
        out_specs=pl.BlockSpec((tm, kd), lambda i: (i, 0)),
        out_shape=jax.ShapeDtypeStruct((m, kd), F32),
        compiler_params=_params(("arbitrary",)),
        name="gla_gate",
    )(h, w_low, w_gk2, b_gk.reshape(-1, 1, kd))


def _chunk_cumsum(lf):
    g, c, kdim = lf.shape
    y = lf.reshape(g * c // SUBLANES, SUBLANES, kdim)
    row = lax.broadcasted_iota(jnp.int32, y.shape, 1)
    shift = 1
    while shift < SUBLANES:
        y = y + jnp.where(row >= shift, pltpu.roll(y, shift, 1), 0.0)
        shift *= 2
    y = y.reshape(g, c // SUBLANES, SUBLANES, kdim)
    parts = [y[:, 0]]
    for i in range(1, c // SUBLANES):
        parts.append(y[:, i] + parts[-1][:, SUBLANES - 1:SUBLANES, :])
    return jnp.stack(parts, axis=1).reshape(g, c, kdim)


def _chunks_local(q, k, v, lf, valid):
    g, c, kdim = q.shape
    b = _chunk_cumsum(lf)
    col =lax.broadcasted_iota(jnp.int32, (g, c, c), 2)
    att = jnp.zeros((g, c, c), F32)
    for j in range(valid):
        w = q * k[:, j:j + 1, :] * jnp.exp(jnp.minimum(b - b[:, j:j + 1, :], 0.0))
        att = jnp.where(col == j, jnp.sum(w, axis=2, keepdims=True), att)
    att = jnp.where(lax.broadcasted_iota(jnp.int32, (g, c, c), 1) >= col, att, 0.0)
    vb = v.astype(BF16)
    o = jnp.einsum("gts,gsv->gtv", att.astype(BF16), vb, preferred_element_type=F32)
    b_last = b[:, c - 1:c, :]
    qd = (q * jnp.exp(b)).astype(BF16)
    kd = (k * jnp.exp(b_last - b)).astype(BF16)
    upd = jnp.einsum("gtk,gtv->gkv", kd, vb, preferred_element_type=F32)
    return o, qd, upd, jnp.exp(b_last)


def _decay_columns(e_row, vdim):
    kdim = e_row.shape[-1]
    e_col = jnp.transpose(jnp.broadcast_to(e_row, (LANES, kdim)))
    if vdim > LANES:
        e_col = jnp.concatenate([e_col] * (vdim // LANES), axis=1)
    return e_col


def _gated_out(o, g, nw):
    ms = jnp.mean(o * o, axis=-1, keepdims=True)
    return (o * lax.rsqrt(ms + EPS) * nw * _silu(g)).astype(BF16)


def _rec_kernel(*refs, mixer, layer_j, has_s0, per_seq_state, gsz, valid, n_sb, n_tblocks):
    refs = list(refs)
    if mixer == "hgrn":
        a_ref, f_ref, v_ref, g_ref, lb_ref, nw_ref = refs[:6]
    else:
        a_ref, k_ref, v_ref, g_ref, f_ref, nw_ref = refs[:6]
    refs = refs[6:]
    s0_ref = refs.pop(0) if has_s0 else None
    og_ref, so_ref = refs[:2]
    scr = refs[2:]
    kdim = a_ref.shape[-1]
    vdim = v_ref.shape[-1]
    ti = pl.program_id(2)

    if mixer == "hgrn":
        raw = lb_ref[...]
        ex = jnp.exp(raw - jnp.max(raw, axis=0, keepdims=True))
        p = ex / jnp.sum(ex, axis=0, keepdims=True)
        cum = p[0:1, :]
        for r in range(1, layer_j + 1):
            cum = cum + p[r:r + 1, :]
        lb = jnp.clip(cum - p[0:1, :], 0.0, 1.0 - 1e-6)
        log_lb = jnp.log(lb)
        log_1m = jnp.log1p(-lb)
        one_m = 1.0 - lb
    nw = nw_ref[...]

    def fields(load):
        a, v, g = load(a_ref), load(v_ref), load(g_ref)
        if mixer == "hgrn":
            fp = load(f_ref)
            q = _silu(a)
            y = log_1m + _log_sigmoid(fp)
            lf = jnp.maximum(log_lb, y) + jnp.log1p(jnp.exp(-jnp.abs(log_lb - y)))
            k = one_m * _sigmoid(-fp)
        else:
            q = a * (kdim ** -0.5)
            k = load(k_ref)
            lf = load(f_ref)
        return q, k, v, lf, g

    if per_seq_state:
        pq, pk, pf, pv = scr
        for r in scr:
            r[...] = jnp.zeros(r.shape, F32)
        q, k, v, lf, g = fields(lambda r: r[...])
        pq[:, 0:valid, :] = q
        pk[:, 0:valid, :] = k
        pf[:, 0:valid, :] = lf
        pv[:, 0:valid, :] = v
        o, qd, upd, e_last = _chunks_local(pq[...], pk[...], pv[...], pf[...], valid)
        outs = []
        for i in range(gsz):
            s = s0_ref[i]
            outs.append(o[i, 0:valid, :] + jnp.dot(qd[i], s.astype(BF16), preferred_element_type=F32)[0:valid, :])
            so_ref[i] = s * _decay_columns(e_last[i], vdim) + upd[i]
        og_ref[...] = _gated_out(jnp.stack(outs), g, nw)
        return

    (s_scr,) = scr

    @pl.when(ti == 0)
    def _():
        if has_s0:
            s_scr[...] = s0_ref[0]
        else:
            s_scr[...] = jnp.zeros(s_scr.shape, F32)

    rows_sb = gsz * CHUNK

    def super_block(sb, carry):
        rows = pl.ds(pl.multiple_of(sb * rows_sb, rows_sb), rows_sb)
        q, k, v, lf, g = fields(lambda r: r[0, rows, :].reshape(gsz, CHUNK, r.shape[-1]))
        o, qd, upd, e_last = _chunks_local(q, k, v, lf, CHUNK)
        s = s_scr[...]
        outs = []
        for i in range(gsz):
            outs.append(o[i] + jnp.dot(qd[i], s.astype(BF16), preferred_element_type=F32))
            s = s * _decay_columns(e_last[i], vdim) + upd[i]
        s_scr[...] = s
        og_ref[0, rows, :] = _gated_out(jnp.stack(outs), g, nw).reshape(rows_sb, vdim)
        return carry

    lax.fori_loop(0, n_sb, super_block, 0, unroll=True)

    @pl.when(ti == n_tblocks - 1)
    def _():
        so_ref[0] = s_scr[...]


def _recurrence(mixer, layer_j, n_layers, proj3, lf3, lb_raw, norm_w, state, state_out, heads, kdim, vdim,
                bb, tb, gsz):
    bsz, t, _ = proj3.shape
    has_s0 = state is not None
    per_seq_state = t < CHUNK
    n_tblocks = t // tb
    kb, vb = kdim, vdim
    nk = heads * kdim // kb
    nv = heads * vdim // vb

    def tok(width, base):
        return pl.BlockSpec((bb, tb, width), lambda b, h, ti: (b, ti, base + h))

    nw_spec = pl.BlockSpec((None, 1, vb), lambda b, h, ti: (layer_j, 0, h))
    nw3 = norm_w.reshape(norm_w.shape[0], 1, -1)
    if mixer == "hgrn":
        in_specs = [tok(kb, 0), tok(kb, nk), tok(vb, 2 * nk), tok(vb, 3 * nk),
                    pl.BlockSpec((lb_raw.shape[0], kb), lambda b, h, ti: (0, h)), nw_spec]
        args = [proj3, proj3, proj3, proj3, lb_raw, nw3]
    else:
        v_base = 2 * heads * kdim // vb
        in_specs = [tok(kb, 0), tok(kb, nk), tok(vb, v_base), tok(vb, v_base + nv), tok(kb, 0), nw_spec]
        args = [proj3, proj3, proj3, proj3, lf3, nw3]
    if has_s0:
        in_specs.append(pl.BlockSpec((None, bb, None, kdim, vdim), lambda b, h, ti: (layer_j, b, h, 0, 0)))
        args.append(state)
    aliases = {}
    if state_out is not None:
        in_specs.append(pl.BlockSpec(memory_space=pl.ANY))
        args.append(state_out)
        aliases = {len(args) - 1: 1}
    if per_seq_state:
        assert has_s0 and tb == t and gsz == bb
        scratch = [pltpu.VMEM((bb, CHUNK, kdim), F32)] * 3 + [pltpu.VMEM((bb, CHUNK, vdim), F32)]
        valid, n_sb = t, 1
    else:
        assert bb == 1 and tb % (gsz * CHUNK) == 0
        scratch = [pltpu.VMEM((kdim, vdim), F32)]
        valid, n_sb = CHUNK, tb // (gsz * CHUNK)

    def kern(*refs):
        if state_out is not None:
            refs = refs[:len(args) - 1] + refs[len(args):]
        _rec_kernel(*refs, mixer=mixer, layer_j=layer_j, has_s0=has_s0, per_seq_state=per_seq_state, gsz=gsz,
                    valid=valid, n_sb=n_sb, n_tblocks=n_tblocks)

    return pl.pallas_call(
        kern,
        grid=(bsz // bb, heads, n_tblocks),
        in_specs=in_specs,
        out_specs=[pl.BlockSpec((bb, tb, vb), lambda b, h, ti: (b, ti, h)),
                   pl.BlockSpec((None, bb, None, kdim, vdim), lambda b, h, ti: (layer_j, b, h, 0, 0))],
        out_shape=[jax.ShapeDtypeStruct((bsz, t, heads * vdim), BF16),
                   jax.ShapeDtypeStruct((n_layers, bsz, heads, kdim, vdim), F32)],
        scratch_shapes=scratch,
        input_output_aliases=aliases,
        compiler_params=_params(("arbitrary", "arbitrary", "arbitrary"), VMEM_LIMIT),
        name=mixer + "_rec",
    )(*args)


def _outproj_kernel(og_ref, w_ref, x_ref, g_ref, sc_ref, sh_ref, lnw_ref, lnb_ref, xo_ref, ho_ref, *, alpha):
    y = jnp.dot(og_ref[...], w_ref[...], preferred_element_type=F32)
    xn = _layer_norm(alpha * x_ref[...] + (1.0 + g_ref[...]) * y, lnw_ref[...], lnb_ref[...])
    xo_ref[...] = xn
    ho_ref[...] = (xn * (1.0 + sc_ref[...]) + sh_ref[...]).astype(BF16)


def _ln_specs(layer, sub, d, n_grid):
    idx = (lambda i: (layer * 2 + sub, 0, 0)) if n_grid == 1 else (lambda i, f: (layer * 2 + sub, 0, 0))
    return [pl.BlockSpec((None, 1, d), idx), pl.BlockSpec((None, 1, d), idx)]


def _outproj(grp, og, w_out, layer_j, x, layer, ln_w, ln_b, alpha):
    m, d = grp.m, grp.d
    sub = grp.with_tile(min(grp.tm, 256))
    tm = sub.tm
    din = w_out.shape[1]
    row = lambda i: (i, 0)
    return pl.pallas_call(
        functools.partial(_outproj_kernel, alpha=alpha),
        grid=(m // tm,),
        in_specs=[pl.BlockSpec((tm, din), row), pl.BlockSpec((None, din, d), lambda i: (layer_j, 0, 0)),
                  pl.BlockSpec((tm, d), row), sub.mod_spec(layer, 2), sub.mod_spec(layer, 4),
                  sub.mod_spec(layer, 3)] + _ln_specs(layer, 0, d, 1),
        out_specs=[pl.BlockSpec((tm, d), row), pl.BlockSpec((tm, d), row)],
        out_shape=[jax.ShapeDtypeStruct((m, d), F32), jax.ShapeDtypeStruct((m, d), BF16)],
        compiler_params=_params(("arbitrary",), VMEM_LIMIT),
        name="out_proj_ln",
    )(og, w_out, x, grp.mod, grp.mod, grp.mod, ln_w, ln_b)


def _mlp_kernel(*refs, alpha, n_f, emit_h):
    if emit_h:
        h_ref, wu_ref, wd_ref, x_ref, g_ref, sc_ref, sh_ref, lnw_ref, lnb_ref, xo_ref, ho_ref, acc_ref = refs
    else:
        h_ref, wu_ref, wd_ref, x_ref, g_ref, lnw_ref, lnb_ref, xo_ref, acc_ref = refs
    f = pl.program_id(1)
    u = jnp.dot(h_ref[...], wu_ref[...], preferred_element_type=F32)
    u = jnp.square(jnp.maximum(u, 0.0)).astype(BF16)
    part = jnp.dot(u, wd_ref[...], preferred_element_type=F32)

    @pl.when(f == 0)
    def _():
        acc_ref[...] = part

    @pl.when(f > 0)
    def _():
        acc_ref[...] += part

    @pl.when(f == n_f - 1)
    def _():
        xn = _layer_norm(alpha * x_ref[...] + (1.0 + g_ref[...]) * acc_ref[...], lnw_ref[...], lnb_ref[...])
        xo_ref[...] = xn
        if emit_h:
            ho_ref[...] = (xn * (1.0 + sc_ref[...]) + sh_ref[...]).astype(BF16)


def _mlp(grp, h, w_up, w_down, x, layer, ln_w, ln_b, alpha, emit_h):
    m, d, tm = grp.m, grp.d, grp.tm
    dff = w_up.shape[-1]
    tf = 512
    n_f = dff // tf
    row = lambda i, f: (i, 0)
    in_specs = [pl.BlockSpec((tm, d), row), pl.BlockSpec((None, d, tf), lambda i, f: (layer, 0, f)),
                pl.BlockSpec((None, tf, d), lambda i, f: (layer, f, 0)), pl.BlockSpec((tm, d), row),
                grp.mod_spec(layer, 5)]
    args = [h, w_up, w_down, x, grp.mod]
    out_specs = [pl.BlockSpec((tm, d), row)]
    out_shape = [jax.ShapeDtypeStruct((m, d), F32)]
    if emit_h:
        in_specs += [grp.mod_spec(layer + 1, 1), grp.mod_spec(layer + 1, 0)]
        args += [grp.mod, grp.mod]
        out_specs.append(pl.BlockSpec((tm, d), row))
        out_shape.append(jax.ShapeDtypeStruct((m, d), BF16))
    in_specs += _ln_specs(layer, 1, d, 2)
    args += [ln_w, ln_b]
    outs = pl.pallas_call(
        functools.partial(_mlp_kernel, alpha=alpha, n_f=n_f, emit_h=emit_h),
        grid=(m // tm, n_f),
        in_specs=in_specs,
        out_specs=out_specs,
        out_shape=out_shape,
        scratch_shapes=[pltpu.VMEM((tm, d), F32)],
        compiler_params=_params(("arbitrary", "arbitrary"), VMEM_LIMIT),
        name="mlp_ln",
    )(*args)
    return (outs[0], outs[1]) if emit_h else (outs[0], None)


def _trunk(grp, x, s_hg, s_gla, w, rec_cfg):
    depth = w["w_up"].shape[0]
    alpha = (2.0 * depth) ** 0.25
    hg_heads, hg_k, hg_v = w["hg_dims"]
    gla_heads, gla_k, gla_v = w["gla_dims"]
    n_hg, n_gla = w["hg_w_out"].shape[0], w["gla_w_out"].shape[0]
    new_hg = new_gla = None
    h = _mod0(grp, x)
    for l in range(depth):
        j = l // 2
        if l % 2 == 0:
            proj = _inproj(grp, h, w["hg_w_in"], j, w["hg_w_in"].shape[-1])
            og, new_hg = _recurrence("hgrn", j, n_hg, proj.reshape(grp.batch, grp.seq, -1), None, w["hg_lb_raw"],
                                     w["hg_norm_w"], s_hg, new_hg, hg_heads, hg_k, hg_v,
                                     rec_cfg["hg_bb"], rec_cfg["tb"], rec_cfg["hg_gsz"])
            w_out = w["hg_w_out"]
        else:
            proj = _inproj(grp, h, w["gla_w_in"], j, w["gla_main"])
            lf = _gla_gate(grp, h, w["gla_w_low"], w["gla_w_gk2"], w["gla_b_gk"], j)
            og, new_gla = _recurrence("gla", j, n_gla, proj.reshape(grp.batch, grp.seq, -1),
                                      lf.reshape(grp.batch, grp.seq, -1), None, w["gla_norm_w"], s_gla, new_gla,
                                      gla_heads, gla_k, gla_v, rec_cfg["gla_bb"], rec_cfg["tb"], rec_cfg["gla_gsz"])
            w_out = w["gla_w_out"]
        x, h2 = _outproj(grp, og.reshape(grp.m, -1), w_out, j, x, l, w["ln_w"], w["ln_b"], alpha)
        x, h = _mlp(grp, h2, w["w_up"], w["w_down"], x, l, w["ln_w"], w["ln_b"], alpha, emit_h=(l + 1 < depth))
    return x, new_hg, new_gla


def kernel(x_prompt, x_sample, state_hgrn, state_gla, c_prompt, c_sample, w_ada, b_ada, ln_w, ln_b,
           hg_w_in, hg_lb_raw, hg_norm_w, hg_w_out, gla_w_in, gla_w_gk2, gla_b_gk, gla_norm_w,
           gla_w_out, w_up, w_down):
    bp, tp, d = x_prompt.shape
    bs, ts, _ = x_sample.shape
    depth = w_up.shape[0]
    _, _, hg_heads, hg_k, hg_v = state_hgrn.shape
    _, _, gla_heads, gla_k, gla_v = state_gla.shape
    gla_main = 2 * gla_heads * gla_k + 2 * gla_heads * gla_v
    rank = gla_w_in.shape[-1] - gla_main

    ms = bs * ts
    pad_rows = (-(ms + bp)) % 8
    c_all = jnp.concatenate([jnp.repeat(c_sample, ts, axis=0), c_prompt, jnp.zeros((pad_rows, d), F32)], axis=0)
    mod = _ada(c_all, w_ada, b_ada)
    mod_p = mod[:, ms:ms + bp].reshape(depth, bp, 1, 6 * d)

    gla_w_in_b = gla_w_in.astype(BF16)
    w = {
        "hg_w_in": hg_w_in.astype(BF16), "hg_lb_raw": hg_lb_raw, "hg_norm_w": hg_norm_w,
        "hg_w_out": hg_w_out.astype(BF16),
        "gla_w_in": gla_w_in_b, "gla_main": gla_main,
        "gla_w_low": jnp.pad(gla_w_in_b[..., gla_main:], ((0, 0), (0, 0), (0, LANES - rank))),
        "gla_w_gk2": jnp.pad(gla_w_gk2, ((0, 0), (0, LANES - rank), (0, 0))).astype(BF16),
        "gla_b_gk": gla_b_gk, "gla_norm_w": gla_norm_w, "gla_w_out": gla_w_out.astype(BF16),
        "w_up": w_up.astype(BF16), "w_down": w_down.astype(BF16),
        "ln_w": ln_w.reshape(depth * 2, 1, d), "ln_b": ln_b.reshape(depth * 2, 1, d),
        "hg_dims": (hg_heads, hg_k, hg_v), "gla_dims": (gla_heads, gla_k, gla_v),
    }

    grp_p = _Group(bp, tp, d, 512, mod_p, per_token=False)
    grp_s = _Group(bs, ts, d, 512, mod, per_token=True)

    y_p, hg_p, gla_p = _trunk(grp_p, x_prompt.reshape(bp * tp, d), None, None, w,
                              {"hg_bb": 1, "gla_bb": 1, "tb": 512, "hg_gsz": 8, "gla_gsz": 4})
    y_s, hg_s, gla_s = _trunk(grp_s, x_sample.reshape(ms, d), state_hgrn, state_gla, w,
                              {"hg_bb": 16, "gla_bb": 4, "tb": ts, "hg_gsz": 16, "gla_gsz": 4})
    return (y_p.reshape(bp, tp, d), y_s.reshape(bs, ts, d), hg_p, gla_p, hg_s, gla_s)
```

```python
import functools

import jax
import jax.numpy as jnp
from jax import lax
from jax.experimental import pallas as pl
from jax.experimental.pallas import tpu as pltpu

F32 = jnp.float32
BF16 = jnp.bfloat16

EPS = 1e-5
GLA_GATE_NORM = 16.0
LANES = 128
SUBLANES = 8
CHUNK = 16
VMEM_LIMIT = 56 * 1024 * 1024


def _sigmoid(x):
    return 1.0 / (1.0 + jnp.exp(-x))


def _silu(x):
    return x * _sigmoid(x)


def _log_sigmoid(x):
    return jnp.minimum(x, 0.0) - jnp.log1p(jnp.exp(-jnp.abs(x)))


def _layer_norm(z, w, b):
    mu = jnp.mean(z, axis=-1, keepdims=True)
    zc = z - mu
    var = jnp.mean(zc * zc, axis=-1, keepdims=True)
    return zc * lax.rsqrt(var + EPS) * w + b


def _params(sem, vmem=None):
    return pltpu.CompilerParams(dimension_semantics=sem, vmem_limit_bytes=vmem)


def _ada_kernel(c_ref, w_ref, b_ref, o_ref):
    cs = _silu(c_ref[...]).astype(BF16)
    o_ref[...] = jnp.dot(cs, w_ref[...].astype(BF16), preferred_element_type=F32) + b_ref[...]


def _ada(c_all, w_ada, b_ada):
    depth, d, n6 = w_ada.shape
    rows = c_all.shape[0]
    tn = 1024
    return pl.pallas_call(
        _ada_kernel,
        grid=(depth, n6 // tn),
        in_specs=[pl.BlockSpec((rows, d), lambda l, j: (0, 0)),
                  pl.BlockSpec((None, d, tn), lambda l, j: (l, 0, j)),
                  pl.BlockSpec((None, 1, tn), lambda l, j: (l, 0, j))],
        out_specs=pl.BlockSpec((None, rows, tn), lambda l, j: (l, 0, j)),
        out_shape=jax.ShapeDtypeStruct((depth, rows, n6), F32),
        compiler_params=_params(("arbitrary", "arbitrary"), VMEM_LIMIT),
        name="ada_proj",
    )(c_all, w_ada, b_ada.reshape(depth, 1, n6))


class _Group:
    def __init__(self, batch, seq, d, tm, mod, per_token):
        self.batch, self.seq, self.d, self.tm = batch, seq, d, tm
        self.m = batch * seq
        self.mod = mod
        self.per_token = per_token

    def with_tile(self, tm):
        return _Group(self.batch, self.seq, self.d, tm, self.mod, self.per_token)

    def mod_spec(self, layer, chunk):
        d, tm = self.d, self.tm
        if self.per_token:
            return pl.BlockSpec((None, tm, d), lambda i, *_: (layer, i, chunk))
        tiles_per_b = self.seq // tm
        return pl.BlockSpec((None, None, 1, d), lambda i, *_: (layer, i // tiles_per_b, 0, chunk))


def _mod0_kernel(x_ref, sc_ref, sh_ref, h_ref):
    h_ref[...] = (x_ref[...] * (1.0 + sc_ref[...]) + sh_ref[...]).astype(BF16)


def _mod0(grp, x):
    m, d, tm = grp.m, grp.d, grp.tm
    return pl.pallas_call(
        _mod0_kernel,
        grid=(m // tm,),
        in_specs=[pl.BlockSpec((tm, d), lambda i: (i, 0)), grp.mod_spec(0, 1), grp.mod_spec(0, 0)],
        out_specs=pl.BlockSpec((tm, d), lambda i: (i, 0)),
        out_shape=jax.ShapeDtypeStruct((m, d), BF16),
        compiler_params=_params(("arbitrary",)),
        name="mod0",
    )(x, grp.mod, grp.mod)


def _mm_kernel(h_ref, w_ref, o_ref):
    o_ref[...] = jnp.dot(h_ref[...], w_ref[...], preferred_element_type=F32)


def _inproj(grp, h, w, layer, n):
    m, d, tm = grp.m, grp.d, grp.tm
    tn = 1024
    return pl.pallas_call(
        _mm_kernel,
        grid=(n // tn, m // tm),
        in_specs=[pl.BlockSpec((tm, d), lambda j, i: (i, 0)),
                  pl.BlockSpec((None, d, tn), lambda j, i: (layer, 0, j))],
        out_specs=pl.BlockSpec((tm, tn), lambda j, i: (i, j)),
        out_shape=jax.ShapeDtypeStruct((m, n), F32),
        compiler_params=_params(("arbitrary", "arbitrary"), VMEM_LIMIT),
        name="in_proj",
    )(h, w)


def _inproj_heads(grp, h, w_heads, layer):
    m, d, tm = grp.m, grp.d, grp.tm
    _, heads, _, width = w_heads.shape
    return pl.pallas_call(
        _mm_kernel,
        grid=(heads, m // tm),
        in_specs=[pl.BlockSpec((tm, d), lambda j, i: (i, 0)),
                  pl.BlockSpec((None, None, d, width), lambda j, i: (layer, j, 0, 0))],
        out_specs=pl.BlockSpec((tm, width), lambda j, i: (i, j)),
        out_shape=jax.ShapeDtypeStruct((m, heads * width), F32),
        compiler_params=_params(("arbitrary", "arbitrary"), VMEM_LIMIT),
        name="in_proj_heads",
    )(h, w_heads)


def _gate_kernel(h_ref, wl_ref, w2_ref, b_ref, o_ref):
    low = jnp.dot(h_ref[...], wl_ref[...], preferred_element_type=F32)
    gk = jnp.dot(low.astype(BF16), w2_ref[...], preferred_element_type=F32) + b_ref[...]
    o_ref[...] = _log_sigmoid(gk) * (1.0 / GLA_GATE_NORM)


def _gla_gate(grp, h, w_low, w_gk2, b_gk, layer):
    m, d, tm = grp.m, grp.d, grp.tm
    kd = w_gk2.shape[-1]
    return pl.pallas_call(
        _gate_kernel,
        grid=(m // tm,),
        in_specs=[pl.BlockSpec((tm, d), lambda i: (i, 0)),
                  pl.BlockSpec((None, d, LANES), lambda i: (layer, 0, 0)),
                  pl.BlockSpec((None, LANES, kd), lambda i: (layer, 0, 0)),
                  pl.BlockSpec((None, 1, kd), lambda i: (layer, 0, 0))],
        out_specs=pl.BlockSpec((tm, kd), lambda i: (i, 0)),
        out_shape=jax.ShapeDtypeStruct((m, kd), F32),
        compiler_params=_params(("arbitrary",)),
        name="gla_gate",
    )(h, w_low, w_gk2, b_gk.reshape(-1, 1, kd))


def _chunk_cumsum(lf):
    g, c, kdim = lf.shape
    y = lf.reshape(g * c // SUBLANES, SUBLANES, kdim)
    row = lax.broadcasted_iota(jnp.int32, y.shape, 1)
    shift = 1
    while shift < SUBLANES:
        y = y + jnp.where(row >= shift, pltpu.roll(y, shift, 1), 0.0)
        shift *= 2
    y = y.reshape(g, c // SUBLANES, SUBLANES, kdim)
    parts = [y[:, 0]]
    for i in range(1, c // SUBLANES):
        parts.append(y[:, i] + parts[-1][:, SUBLANES - 1:SUBLANES, :])
    return jnp.stack(parts, axis=1).reshape(g, c, kdim)


def _chunks_local(q, k, v, lf, valid):
    g, c, kdim = q.shape
    assert c == 2 * SUBLANES
    b = _chunk_cumsum(lf)
    g2 = 2 * g
    q8, k8, b8 = (x.reshape(g2, SUBLANES, kdim) for x in (q, k, b))
    col = lax.broadcasted_iota(jnp.int32, (g2, SUBLANES, c), 2)
    col0 = (lax.broadcasted_iota(jnp.int32, (g2, SUBLANES, c), 0) % 2) * SUBLANES
    att = jnp.zeros((g2, SUBLANES, c), F32)
    for j in range(min(valid, SUBLANES)):
        w = q8 * k8[:, j:j + 1, :] * jnp.exp(jnp.minimum(b8 - b8[:, j:j + 1, :], 0.0))
        att = jnp.where(col == col0 + j, jnp.sum(w, axis=2, keepdims=True), att)
    att = jnp.where(col - col0 <= lax.broadcasted_iota(jnp.int32, (g2, SUBLANES, c), 1), att, 0.0)
    att = att.reshape(g, c, c)
    if valid > SUBLANES:
        r = b[:, SUBLANES - 1:SUBLANES, :]
        q_hi = (q * jnp.exp(jnp.minimum(b - r, 0.0))).astype(BF16)
        k_lo = (k * jnp.exp(jnp.minimum(r - b, 0.0))).astype(BF16)
        cross = jnp.einsum("gtk,gsk->gts", q_hi, k_lo, preferred_element_type=F32)
        row16 = lax.broadcasted_iota(jnp.int32, (g, c, c), 1)
        col16 = lax.broadcasted_iota(jnp.int32, (g, c, c), 2)
        att = jnp.where((row16 >= SUBLANES) & (col16 < SUBLANES), cross, att)
    vb = v.astype(BF16)
    o = jnp.einsum("gts,gsv->gtv", att.astype(BF16), vb, preferred_element_type=F32)
    b_last = b[:, c - 1:c, :]
    qd = (q * jnp.exp(b)).astype(BF16)
    kd = (k * jnp.exp(b_last - b)).astype(BF16)
    upd = jnp.einsum("gtk,gtv->gkv", kd, vb, preferred_element_type=F32)
    return o, qd, upd, jnp.exp(b_last)


def _decay_columns(e_row, vdim):
    kdim = e_row.shape[-1]
    e_col = jnp.transpose(jnp.broadcast_to(e_row, (LANES, kdim)))
    if vdim > LANES:
        e_col = jnp.concatenate([e_col] * (vdim // LANES), axis=1)
    return e_col


def _gated_out(o, g, nw):
    ms = jnp.mean(o * o, axis=-1, keepdims=True)
    return (o * lax.rsqrt(ms + EPS) * nw * _silu(g)).astype(BF16)


def _rec_kernel(*refs, mixer, fused, layer_j, has_s0, per_seq_state, gsz, valid, n_sb, n_tblocks):
    refs = list(refs)
    if fused:
        h_ref, w_ref, lb_ref, nw_ref = refs[:4]
        refs = refs[4:]
    elif mixer == "hgrn":
        a_ref, f_ref, v_ref, g_ref, lb_ref, nw_ref = refs[:6]
        refs = refs[6:]
    else:
        a_ref, k_ref, v_ref, g_ref, f_ref, nw_ref = refs[:6]
        refs = refs[6:]
    s0_ref = refs.pop(0) if has_s0 else None
    og_ref, so_ref = refs[:2]
    scr = refs[2:]
    kdim, vdim = so_ref.shape[-2:]
    ti = pl.program_id(2)

    if mixer == "hgrn":
        raw = lb_ref[...]
        ex = jnp.exp(raw - jnp.max(raw, axis=0, keepdims=True))
        p = ex / jnp.sum(ex, axis=0, keepdims=True)
        cum = p[0:1, :]
        for r in range(1, layer_j + 1):
            cum = cum + p[r:r + 1, :]
        lb = jnp.clip(cum - p[0:1, :], 0.0, 1.0 - 1e-6)
        log_lb = jnp.log(lb)
        log_1m = jnp.log1p(-lb)
        one_m = 1.0 - lb
    nw = nw_ref[...]

    def fields(load):
        if fused:
            proj = load(None)
            a, fp, v, g = (proj[..., i * kdim:(i + 1) * kdim] for i in range(4))
        else:
            a, v, g = load(a_ref), load(v_ref), load(g_ref)
        if mixer == "hgrn":
            if not fused:
                fp = load(f_ref)
            q = _silu(a)
            y = log_1m + _log_sigmoid(fp)
            lf = jnp.maximum(log_lb, y) + jnp.log1p(jnp.exp(-jnp.abs(log_lb - y)))
            k = one_m * _sigmoid(-fp)
        else:
            q = a * (kdim ** -0.5)
            k = load(k_ref)
            lf = load(f_ref)
        return q, k, v, lf, g

    if per_seq_state:
        pq, pk, pf, pv = scr
        for r in scr:
            r[...] = jnp.zeros(r.shape, F32)
        q, k, v, lf, g = fields(lambda r: r[...])
        pq[:, 0:valid, :] = q
        pk[:, 0:valid, :] = k
        pf[:, 0:valid, :] = lf
        pv[:, 0:valid, :] = v
        o, qd, upd, e_last = _chunks_local(pq[...], pk[...], pv[...], pf[...], valid)
        outs = []
        for i in range(gsz):
            s = s0_ref[i]
            outs.append(o[i, 0:valid, :] + jnp.dot(qd[i], s.astype(BF16), preferred_element_type=F32)[0:valid, :])
            so_ref[i] = s * _decay_columns(e_last[i], vdim) + upd[i]
        og_ref[...] = _gated_out(jnp.stack(outs), g, nw)
        return

    s_scr = scr[0]

    @pl.when(ti == 0)
    def _():
        if has_s0:
            s_scr[...] = s0_ref[0]
        else:
            s_scr[...] = jnp.zeros(s_scr.shape, F32)

    rows_sb = gsz * CHUNK

    def time_block(src_ref):
        for sb in range(n_sb):
            rows = pl.ds(sb * rows_sb, rows_sb)

            def load(r):
                x = src_ref[rows, :] if r is None else r[0, rows, :]
                return x.reshape(gsz, CHUNK, x.shape[-1])

            q, k, v, lf, g = fields(load)
            o, qd, upd, e_last = _chunks_local(q, k, v, lf, CHUNK)
            s = s_scr[...]
            outs = []
            for i in range(gsz):
                outs.append(o[i] + jnp.dot(qd[i], s.astype(BF16), preferred_element_type=F32))
                s = s * _decay_columns(e_last[i], vdim) + upd[i]
            s_scr[...] = s
            og_ref[0, rows, :] = _gated_out(jnp.stack(outs), g, nw).reshape(rows_sb, vdim)

    if not fused:
        time_block(None)

        @pl.when(ti == n_tblocks - 1)
        def _():
            so_ref[0] = s_scr[...]
        return

    proj_a, proj_b = scr[1:]

    def project(dst_ref):
        dst_ref[...] = jnp.dot(h_ref[0], w_ref[...], preferred_element_type=F32)

    @pl.when(ti == 0)
    def _():
        project(proj_a)

    odd = lax.rem(ti, 2) == 1

    @pl.when(odd)
    def _():
        project(proj_b)
        time_block(proj_a)

    @pl.when(jnp.logical_and(jnp.logical_not(odd), ti > 0))
    def _():
        project(proj_a)
        time_block(proj_b)

    @pl.when(ti == n_tblocks)
    def _():
        so_ref[0] = s_scr[...]


def _recurrence(mixer, layer_j, n_layers, proj3, lf3, lb_raw, norm_w, state, state_out, heads, kdim, vdim,
                bb, tb, gsz, w_heads=None):
    bsz, t, _ = proj3.shape
    has_s0 = state is not None
    per_seq_state = t < CHUNK
    fused = w_heads is not None
    n_tblocks = t // tb
    kb, vb = kdim, vdim
    nk = heads * kdim // kb
    nv = heads * vdim // vb

    def tok(width, base, stride=1):
        return pl.BlockSpec((bb, tb, width), lambda b, h, ti: (b, ti, base + stride * h))

    nw_spec = pl.BlockSpec((None, 1, vb), lambda b, h, ti: (layer_j, 0, h))
    nw3 = norm_w.reshape(norm_w.shape[0], 1, -1)
    lb_spec = None if lb_raw is None else pl.BlockSpec((lb_raw.shape[0], kb), lambda b, h, ti: (0, h))
    if fused:
        assert mixer == "hgrn" and kdim == vdim and not per_seq_state
        d = proj3.shape[-1]
        in_specs = [pl.BlockSpec((bb, tb, d), lambda b, h, ti: (b, jnp.minimum(ti, n_tblocks - 1), 0)),
                    pl.BlockSpec((None, None, d, 4 * kdim), lambda b, h, ti: (layer_j, h, 0, 0)), lb_spec, nw_spec]
        args = [proj3, w_heads, lb_raw, nw3]
    elif mixer == "hgrn":
        assert kdim == vdim
        in_specs = [tok(kb, 0, 4), tok(kb, 1, 4), tok(vb, 2, 4), tok(vb, 3, 4), lb_spec, nw_spec]
        args = [proj3, proj3, proj3, proj3, lb_raw, nw3]
    else:
        v_base = 2 * heads * kdim // vb
        in_specs = [tok(kb, 0), tok(kb, nk), tok(vb, v_base), tok(vb, v_base + nv), tok(kb, 0), nw_spec]
        args = [proj3, proj3, proj3, proj3, lf3, nw3]
    if has_s0:
        in_specs.append(pl.BlockSpec((None, bb, None, kdim, vdim), lambda b, h, ti: (layer_j, b, h, 0, 0)))
        args.append(state)
    aliases = {}
    if state_out is not None:
        in_specs.append(pl.BlockSpec(memory_space=pl.ANY))
        args.append(state_out)
        aliases = {len(args) - 1: 1}
    if per_seq_state:
        assert has_s0 and tb == t and gsz == bb
        scratch = [pltpu.VMEM((bb, CHUNK, kdim), F32)] * 3 + [pltpu.VMEM((bb, CHUNK, vdim), F32)]
        valid, n_sb = t, 1
    else:
        assert bb == 1 and tb % (gsz * CHUNK) == 0
        scratch = [pltpu.VMEM((kdim, vdim), F32)]
        if fused:
            scratch += [pltpu.VMEM((tb, 4 * kdim), F32)] * 2
        valid, n_sb = CHUNK, tb // (gsz * CHUNK)
    n_steps = n_tblocks + 1 if fused else n_tblocks
    og_block = (lambda b, h, ti: (b, jnp.maximum(ti - 1, 0), h)) if fused else (lambda b, h, ti: (b, ti, h))

    def kern(*refs):
        if state_out is not None:
            refs = refs[:len(args) - 1] + refs[len(args):]
        _rec_kernel(*refs, mixer=mixer, fused=fused, layer_j=layer_j, has_s0=has_s0, per_seq_state=per_seq_state,
                    gsz=gsz, valid=valid, n_sb=n_sb, n_tblocks=n_tblocks)

    return pl.pallas_call(
        kern,
        grid=(bsz // bb, heads, n_steps),
        in_specs=in_specs,
        out_specs=[pl.BlockSpec((bb, tb, vb), og_block),
                   pl.BlockSpec((None, bb, None, kdim, vdim), lambda b, h, ti: (layer_j, b, h, 0, 0))],
        out_shape=[jax.ShapeDtypeStruct((bsz, t, heads * vdim), BF16),
                   jax.ShapeDtypeStruct((n_layers, bsz, heads, kdim, vdim), F32)],
        scratch_shapes=scratch,
        input_output_aliases=aliases,
        compiler_params=_params(("arbitrary", "arbitrary", "arbitrary"), VMEM_LIMIT),
        name=mixer + "_rec",
    )(*args)


def _outproj_kernel(og_ref, w_ref, x_ref, g_ref, sc_ref, sh_ref, lnw_ref, lnb_ref, xo_ref, ho_ref, *, alpha):
    y = jnp.dot(og_ref[...], w_ref[...], preferred_element_type=F32)
    xn = _layer_norm(alpha * x_ref[...] + (1.0 + g_ref[...]) * y, lnw_ref[...], lnb_ref[...])
    xo_ref[...] = xn
    ho_ref[...] = (xn * (1.0 + sc_ref[...]) + sh_ref[...]).astype(BF16)


def _ln_specs(layer, sub, d, n_grid):
    idx = (lambda i: (layer * 2 + sub, 0, 0)) if n_grid == 1 else (lambda i, f: (layer * 2 + sub, 0, 0))
    return [pl.BlockSpec((None, 1, d), idx), pl.BlockSpec((None, 1, d), idx)]


def _outproj(grp, og, w_out, layer_j, x, layer, ln_w, ln_b, alpha):
    m, d = grp.m, grp.d
    sub = grp.with_tile(min(grp.tm, 256))
    tm = sub.tm
    din = w_out.shape[1]
    row = lambda i: (i, 0)
    return pl.pallas_call(
        functools.partial(_outproj_kernel, alpha=alpha),
        grid=(m // tm,),
        in_specs=[pl.BlockSpec((tm, din), row), pl.BlockSpec((None, din, d), lambda i: (layer_j, 0, 0)),
                  pl.BlockSpec((tm, d), row), sub.mod_spec(layer, 2), sub.mod_spec(layer, 4),
                  sub.mod_spec(layer, 3)] + _ln_specs(layer, 0, d, 1),
        out_specs=[pl.BlockSpec((tm, d), row), pl.BlockSpec((tm, d), row)],
        out_shape=[jax.ShapeDtypeStruct((m, d), F32), jax.ShapeDtypeStruct((m, d), BF16)],
        compiler_params=_params(("arbitrary",), VMEM_LIMIT),
        name="out_proj_ln",
    )(og, w_out, x, grp.mod, grp.mod, grp.mod, ln_w, ln_b)


def _mlp_kernel(*refs, alpha, n_f, emit_h):
    if emit_h:
        h_ref, wu_ref, wd_ref, x_ref, g_ref, sc_ref, sh_ref, lnw_ref, lnb_ref, xo_ref, ho_ref, acc_ref = refs
    else:
        h_ref, wu_ref, wd_ref, x_ref, g_ref, lnw_ref, lnb_ref, xo_ref, acc_ref = refs
    f = pl.program_id(1)

    @pl.when(f == 0)
    def _():
        acc_ref[...] = jnp.zeros(acc_ref.shape, F32)

    u = jnp.dot(h_ref[...], wu_ref[...], preferred_element_type=F32)
    u = jnp.square(jnp.maximum(u, 0.0)).astype(BF16)
    acc_ref[...] += jnp.dot(u, wd_ref[...], preferred_element_type=F32)

    @pl.when(f == n_f - 1)
    def _():
        xn = _layer_norm(alpha * x_ref[...] + (1.0 + g_ref[...]) * acc_ref[...], lnw_ref[...], lnb_ref[...])
        xo_ref[...] = xn
        if emit_h:
            ho_ref[...] = (xn * (1.0 + sc_ref[...]) + sh_ref[...]).astype(BF16)


def _mlp(grp, h, w_up, w_down, x, layer, ln_w, ln_b, alpha, emit_h):
    m, d, tm = grp.m, grp.d, grp.tm
    dff = w_up.shape[-1]
    tf = 1024
    n_f = dff // tf
    row = lambda i, f: (i, 0)
    in_specs = [pl.BlockSpec((tm, d), row), pl.BlockSpec((None, d, tf), lambda i, f: (layer, 0, f)),
                pl.BlockSpec((None, tf, d), lambda i, f: (layer, f, 0)), pl.BlockSpec((tm, d), row),
                grp.mod_spec(layer, 5)]
    args = [h, w_up, w_down, x, grp.mod]
    out_specs = [pl.BlockSpec((tm, d), row)]
    out_shape = [jax.ShapeDtypeStruct((m, d), F32)]
    if emit_h:
        in_specs += [grp.mod_spec(layer + 1, 1), grp.mod_spec(layer + 1, 0)]
        args += [grp.mod, grp.mod]
        out_specs.append(pl.BlockSpec((tm, d), row))
        out_shape.append(jax.ShapeDtypeStruct((m, d), BF16))
    in_specs += _ln_specs(layer, 1, d, 2)
    args += [ln_w, ln_b]
    outs = pl.pallas_call(
        functools.partial(_mlp_kernel, alpha=alpha, n_f=n_f, emit_h=emit_h),
        grid=(m // tm, n_f),
        in_specs=in_specs,
        out_specs=out_specs,
        out_shape=out_shape,
        scratch_shapes=[pltpu.VMEM((tm, d), F32)],
        compiler_params=_params(("arbitrary", "arbitrary"), VMEM_LIMIT),
        name="mlp_ln",
    )(*args)
    return (outs[0], outs[1]) if emit_h else (outs[0], None)


def _trunk(grp, x, s_hg, s_gla, w, rec_cfg):
    depth = w["w_up"].shape[0]
    alpha = (2.0 * depth) ** 0.25
    hg_heads, hg_k, hg_v = w["hg_dims"]
    gla_heads, gla_k, gla_v = w["gla_dims"]
    n_hg, n_gla = w["hg_w_out"].shape[0], w["gla_w_out"].shape[0]
    new_hg = new_gla = None
    h = _mod0(grp, x)
    for l in range(depth):
        j = l // 2
        if l % 2 == 0:
            if rec_cfg["hg_fused"]:
                src, w_heads = h, w["hg_w_heads"]
            else:
                src, w_heads = _inproj_heads(grp, h, w["hg_w_heads"], j), None
            og, new_hg = _recurrence("hgrn", j, n_hg, src.reshape(grp.batch, grp.seq, -1), None, w["hg_lb_raw"],
                                     w["hg_norm_w"], s_hg, new_hg, hg_heads, hg_k, hg_v,
                                     rec_cfg["hg_bb"], rec_cfg["tb"], rec_cfg["hg_gsz"], w_heads)
            w_out = w["hg_w_out"]
        else:
            proj = _inproj(grp, h, w["gla_w_in"], j, w["gla_main"])
            lf = _gla_gate(grp, h, w["gla_w_low"], w["gla_w_gk2"], w["gla_b_gk"], j)
            og, new_gla = _recurrence("gla", j, n_gla, proj.reshape(grp.batch, grp.seq, -1),
                                      lf.reshape(grp.batch, grp.seq, -1), None, w["gla_norm_w"], s_gla, new_gla,
                                      gla_heads, gla_k, gla_v, rec_cfg["gla_bb"], rec_cfg["tb"], rec_cfg["gla_gsz"])
            w_out = w["gla_w_out"]
        x, h2 = _outproj(grp, og.reshape(grp.m, -1), w_out, j, x, l, w["ln_w"], w["ln_b"], alpha)
        x, h = _mlp(grp, h2, w["w_up"], w["w_down"], x, l, w["ln_w"], w["ln_b"], alpha, emit_h=(l + 1 < depth))
    return x, new_hg, new_gla


def kernel(x_prompt, x_sample, state_hgrn, state_gla, c_prompt, c_sample, w_ada, b_ada, ln_w, ln_b,
           hg_w_in, hg_lb_raw, hg_norm_w, hg_w_out, gla_w_in, gla_w_gk2, gla_b_gk, gla_norm_w,
           gla_w_out, w_up, w_down):
    bp, tp, d = x_prompt.shape
    bs, ts, _ = x_sample.shape
    depth = w_up.shape[0]
    _, _, hg_heads, hg_k, hg_v = state_hgrn.shape
    _, _, gla_heads, gla_k, gla_v = state_gla.shape
    gla_main = 2 * gla_heads * gla_k + 2 * gla_heads * gla_v
    rank = gla_w_in.shape[-1] - gla_main

    ms = bs * ts
    pad_rows = (-(ms + bp)) % 8
    c_all = jnp.concatenate([jnp.repeat(c_sample, ts, axis=0), c_prompt, jnp.zeros((pad_rows, d), F32)], axis=0)
    mod = _ada(c_all, w_ada, b_ada)
    mod_p = mod[:, ms:ms + bp].reshape(depth, bp, 1, 6 * d)

    gla_w_in_b = gla_w_in.astype(BF16)
    n_hg = hg_w_in.shape[0]
    hg_w_heads = (hg_w_in.astype(BF16).reshape(n_hg, d, 4, hg_heads, hg_k)
                  .transpose(0, 3, 1, 2, 4).reshape(n_hg, hg_heads, d, 4 * hg_k))
    w = {
        "hg_w_heads": hg_w_heads, "hg_lb_raw": hg_lb_raw, "hg_norm_w": hg_norm_w,
        "hg_w_out": hg_w_out.astype(BF16),
        "gla_w_in": gla_w_in_b, "gla_main": gla_main,
        "gla_w_low": jnp.pad(gla_w_in_b[..., gla_main:], ((0, 0), (0, 0), (0, LANES - rank))),
        "gla_w_gk2": jnp.pad(gla_w_gk2, ((0, 0), (0, LANES - rank), (0, 0))).astype(BF16),
        "gla_b_gk": gla_b_gk, "gla_norm_w": gla_norm_w, "gla_w_out": gla_w_out.astype(BF16),
        "w_up": w_up.astype(BF16), "w_down": w_down.astype(BF16),
        "ln_w": ln_w.reshape(depth * 2, 1, d), "ln_b": ln_b.reshape(depth * 2, 1, d),
        "hg_dims": (hg_heads, hg_k, hg_v), "gla_dims": (gla_heads, gla_k, gla_v),
    }

    grp_p = _Group(bp, tp, d, 512, mod_p, per_token=False)
    grp_s = _Group(bs, ts, d, 512, mod, per_token=True)

    y_p, hg_p, gla_p = _trunk(grp_p, x_prompt.reshape(bp * tp, d), None, None, w,
                              {"hg_bb": 1, "gla_bb": 1, "tb": 512, "hg_gsz": 8, "gla_gsz": 4, "hg_fused": True})
    y_s, hg_s, gla_s = _trunk(grp_s, x_sample.reshape(ms, d), state_hgrn, state_gla, w,
                              {"hg_bb": 16, "gla_bb": 4, "tb": ts, "hg_gsz": 16, "gla_gsz": 4, "hg_fused": False})
    return (y_p.reshape(bp, tp, d), y_s.reshape(bs, ts, d), hg_p, gla_p, hg_s, gla_s)
```

```python
import functools

import jax
import jax.numpy as jnp
from jax import lax
from jax.experimental import pallas as pl
from jax.experimental.pallas import tpu as pltpu

F32 = jnp.float32
BF16 = jnp.bfloat16

EPS = 1e-5
LOG2E = 1.4426950408889634
GLA_GATE_NORM = 16.0
LANES = 128
SUBLANES = 8
CHUNK = 16
VMEM_LIMIT = 56 * 1024 * 1024


def _sigmoid(x):
    return 1.0 / (1.0 + jnp.exp(-x))


def _silu(x):
    return x * _sigmoid(x)


def _log_sigmoid(x):
    return jnp.minimum(x, 0.0) - jnp.log1p(jnp.exp(-jnp.abs(x)))


def _layer_norm(z, w, b):
    mu = jnp.mean(z, axis=-1, keepdims=True)
    zc = z - mu
    var = jnp.mean(zc * zc, axis=-1, keepdims=True)
    return zc * lax.rsqrt(var + EPS) * w + b


def _params(sem, vmem=None):
    return pltpu.CompilerParams(dimension_semantics=sem, vmem_limit_bytes=vmem)


def _ada_kernel(c_ref, w_ref, b_ref, o_ref):
    cs = _silu(c_ref[...]).astype(BF16)
    o_ref[...] = jnp.dot(cs, w_ref[...].astype(BF16), preferred_element_type=F32) + b_ref[...]


def _ada(c_all, w_ada, b_ada):
    depth, d, n6 = w_ada.shape
    rows = c_all.shape[0]
    tn = 1024
    return pl.pallas_call(
        _ada_kernel,
        grid=(depth, n6 // tn),
        in_specs=[pl.BlockSpec((rows, d), lambda l, j: (0, 0)),
                  pl.BlockSpec((None, d, tn), lambda l, j: (l, 0, j)),
                  pl.BlockSpec((None, 1, tn), lambda l, j: (l, 0, j))],
        out_specs=pl.BlockSpec((None, rows, tn), lambda l, j: (l, 0, j)),
        out_shape=jax.ShapeDtypeStruct((depth, rows, n6), F32),
        compiler_params=_params(("arbitrary", "arbitrary"), VMEM_LIMIT),
        name="ada_proj",
    )(c_all, w_ada, b_ada.reshape(depth, 1, n6))


class _Group:
    def __init__(self, batch, seq, d, tm, mod, per_token):
        self.batch, self.seq, self.d, self.tm = batch, seq, d, tm
        self.m = batch * seq
        self.mod = mod
        self.per_token = per_token

    def with_tile(self, tm):
        return _Group(self.batch, self.seq, self.d, tm, self.mod, self.per_token)

    def mod_spec(self, layer, chunk):
        d, tm = self.d, self.tm
        if self.per_token:
            return pl.BlockSpec((None, tm, d), lambda i, *_: (layer, i, chunk))
        tiles_per_b = self.seq // tm
        return pl.BlockSpec((None, None, 1, d), lambda i, *_: (layer, i // tiles_per_b, 0, chunk))


def _mod0_kernel(x_ref, sc_ref, sh_ref, h_ref):
    h_ref[...] = (x_ref[...] * (1.0 + sc_ref[...]) + sh_ref[...]).astype(BF16)


def _mod0(grp, x):
    m, d, tm = grp.m, grp.d, grp.tm
    return pl.pallas_call(
        _mod0_kernel,
        grid=(m // tm,),
        in_specs=[pl.BlockSpec((tm, d), lambda i: (i, 0)), grp.mod_spec(0, 1), grp.mod_spec(0, 0)],
        out_specs=pl.BlockSpec((tm, d), lambda i: (i, 0)),
        out_shape=jax.ShapeDtypeStruct((m, d), BF16),
        compiler_params=_params(("arbitrary",)),
        name="mod0",
    )(x, grp.mod, grp.mod)


def _mm_kernel(h_ref, w_ref, o_ref):
    o_ref[...] = jnp.dot(h_ref[...], w_ref[...], preferred_element_type=F32)


def _inproj(grp, h, w, layer, n):
    m, d, tm = grp.m, grp.d, grp.tm
    tn = 1024
    return pl.pallas_call(
        _mm_kernel,
        grid=(n // tn, m // tm),
        in_specs=[pl.BlockSpec((tm, d), lambda j, i: (i, 0)),
                  pl.BlockSpec((None, d, tn), lambda j, i: (layer, 0, j))],
        out_specs=pl.BlockSpec((tm, tn), lambda j, i: (i, j)),
        out_shape=jax.ShapeDtypeStruct((m, n), F32),
        compiler_params=_params(("arbitrary", "arbitrary"), VMEM_LIMIT),
        name="in_proj",
    )(h, w)


def _regroup_kernel(*refs):
    *in_refs, o_ref = refs
    width = in_refs[0].shape[-1]
    for f, r in enumerate(in_refs):
        o_ref[:, f * width:(f + 1) * width] = r[...].astype(BF16)


def _regroup_heads(w, heads, n_fields):
    layers, d, n = w.shape
    kdim = n // (n_fields * heads)
    return pl.pallas_call(
        _regroup_kernel,
        grid=(layers, heads),
        in_specs=[pl.BlockSpec((None, d, kdim), functools.partial(lambda l, h, f: (l, 0, f * heads + h), f=f))
                  for f in range(n_fields)],
        out_specs=pl.BlockSpec((None, None, d, n_fields * kdim), lambda l, h: (l, h, 0, 0)),
        out_shape=jax.ShapeDtypeStruct((layers, heads, d, n_fields * kdim), BF16),
        compiler_params=_params(("arbitrary", "arbitrary")),
        name="regroup_heads",
    )(*([w] * n_fields))


def _inproj_heads(grp, h, w_heads, layer):
    m, d, tm = grp.m, grp.d, grp.tm
    _, heads, _, width = w_heads.shape
    return pl.pallas_call(
        _mm_kernel,
        grid=(heads, m // tm),
        in_specs=[pl.BlockSpec((tm, d), lambda j, i: (i, 0)),
                  pl.BlockSpec((None, None, d, width), lambda j, i: (layer, j, 0, 0))],
        out_specs=pl.BlockSpec((tm, width), lambda j, i: (i, j)),
        out_shape=jax.ShapeDtypeStruct((m, heads * width), F32),
        compiler_params=_params(("arbitrary", "arbitrary"), VMEM_LIMIT),
        name="in_proj_heads",
    )(h, w_heads)


def _gate_kernel(h_ref, wl_ref, w2_ref, b_ref, o_ref):
    low = jnp.dot(h_ref[...], wl_ref[...], preferred_element_type=F32)
    gk = jnp.dot(low.astype(BF16), w2_ref[...], preferred_element_type=F32) + b_ref[...]
    o_ref[...] = _log_sigmoid(gk) * (1.0 / GLA_GATE_NORM)


def _gla_gate(grp, h, w_low, w_gk2, b_gk, layer):
    m, d, tm = grp.m, grp.d, grp.tm
    kd = w_gk2.shape[-1]
    return pl.pallas_call(
        _gate_kernel,
        grid=(m // tm,),
        in_specs=[pl.BlockSpec((tm, d), lambda i: (i, 0)),
                  pl.BlockSpec((None, d, LANES), lambda i: (layer, 0, 0)),
                  pl.BlockSpec((None, LANES, kd), lambda i: (layer, 0, 0)),
                  pl.BlockSpec((None, 1, kd), lambda i: (layer, 0, 0))],
        out_specs=pl.BlockSpec((tm, kd), lambda i: (i, 0)),
        out_shape=jax.ShapeDtypeStruct((m, kd), F32),
        compiler_params=_params(("arbitrary",)),
        name="gla_gate",
    )(h, w_low, w_gk2, b_gk.reshape(-1, 1, kd))


def _chunk_cumsum(lf):
    g, c, kdim = lf.shape
    y = lf.reshape(g * c // SUBLANES, SUBLANES, kdim)
    row = lax.broadcasted_iota(jnp.int32, y.shape, 1)
    shift = 1
    while shift < SUBLANES:
        y = y + jnp.where(row >= shift, pltpu.roll(y, shift, 1), 0.0)
        shift *= 2
    y = y.reshape(g, c // SUBLANES, SUBLANES, kdim)
    parts = [y[:, 0]]
    for i in range(1, c // SUBLANES):
        parts.append(y[:, i] + parts[-1][:, SUBLANES - 1:SUBLANES, :])
    return jnp.stack(parts, axis=1).reshape(g, c, kdim)


def _chunk_parts(q, k, v, lf, valid):
    g, c, kdim = q.shape
    assert c == 2 * SUBLANES
    b = _chunk_cumsum(lf)
    g2 = 2 * g
    q8, k8, b8 = (x.reshape(g2, SUBLANES, kdim) for x in (q, k, b * LOG2E))
    col = lax.broadcasted_iota(jnp.int32, (g2, SUBLANES, c), 2)
    col0 = (lax.broadcasted_iota(jnp.int32, (g2, SUBLANES, c), 0) % 2) * SUBLANES
    att = jnp.zeros((g2, SUBLANES, c), F32)
    for j in range(min(valid, SUBLANES)):
        w = q8 * k8[:, j:j + 1, :] * jnp.exp2(jnp.minimum(b8 - b8[:, j:j + 1, :], 0.0))
        att = jnp.where(col == col0 + j, jnp.sum(w, axis=2, keepdims=True), att)
    att = jnp.where(col - col0 <= lax.broadcasted_iota(jnp.int32, (g2, SUBLANES, c), 1), att, 0.0)
    att = att.reshape(g, c, c)
    if valid > SUBLANES:
        r = b[:, SUBLANES - 1:SUBLANES, :]
        q_hi = (q * jnp.exp(jnp.minimum(b - r, 0.0))).astype(BF16)
        k_lo = (k * jnp.exp(jnp.minimum(r - b, 0.0))).astype(BF16)
        cross = jnp.einsum("gtk,gsk->gts", q_hi, k_lo, preferred_element_type=F32)
        row16 = lax.broadcasted_iota(jnp.int32, (g, c, c), 1)
        col16 = lax.broadcasted_iota(jnp.int32, (g, c, c), 2)
        att = jnp.where((row16 >= SUBLANES) & (col16 < SUBLANES), cross, att)
    o = jnp.einsum("gts,gsv->gtv", att.astype(BF16), v.astype(BF16), preferred_element_type=F32)
    b_last = b[:, c - 1:c, :]
    return o, q * jnp.exp(b), k * jnp.exp(b_last - b), b_last


def _superblock_chain(q, k, v, lf, s0):
    g, c, _ = q.shape
    vdim = v.shape[-1]
    o, qd, kd, b_last = _chunk_parts(q, k, v, lf, c)
    qd = qd.astype(BF16)
    upd = jnp.einsum("gtk,gtv->gkv", kd.astype(BF16), v.astype(BF16), preferred_element_type=F32)
    e_last = jnp.exp(b_last)
    s = s0
    outs = []
    for i in range(g):
        outs.append(o[i] + jnp.dot(qd[i], s.astype(BF16), preferred_element_type=F32))
        s = s * _decay_columns(e_last[i], vdim) + upd[i]
    return jnp.stack(outs).reshape(g * c, vdim), s


def _level_masks(g, c):
    rows = g * c
    row_i = lax.broadcasted_iota(jnp.int32, (rows, rows), 0)
    col_i = lax.broadcasted_iota(jnp.int32, (rows, rows), 1)
    masks = []
    bit = c
    while bit < rows:
        masks.append(((row_i ^ col_i) < 2 * bit) & ((row_i & bit) != 0) & ((col_i & bit) == 0))
        bit *= 2
    return masks


def _superblock(q, k, v, lf, s0, masks):
    g, c, kdim = q.shape
    vdim = v.shape[-1]
    rows = g * c
    o, qd, kd, b_last = _chunk_parts(q, k, v, lf, c)
    offs = [jnp.zeros((1, kdim), F32)]
    for i in range(1, g):
        offs.append(offs[-1] + b_last[i - 1])
    off = jnp.stack(offs)
    off_end = off + b_last
    total = off_end[g - 1]
    vb = v.astype(BF16).reshape(rows, vdim)
    att = jnp.zeros((rows, rows), F32)
    half = 1
    for take in masks:
        mid = jnp.stack([offs[(i // (2 * half)) * 2 * half + half] for i in range(g)])
        lhs = (qd * jnp.exp(jnp.minimum(off - mid, 0.0))).astype(BF16).reshape(rows, kdim)
        rhs = (kd * jnp.exp(jnp.minimum(mid - off_end, 0.0))).astype(BF16).reshape(rows, kdim)
        a = lax.dot_general(lhs, rhs, (((1,), (1,)), ((), ())), preferred_element_type=F32)
        att = jnp.where(take, a, att)
        half *= 2
    out = o.reshape(rows, vdim)
    if g > 1:
        out = out + jnp.dot(att.astype(BF16), vb, preferred_element_type=F32)
    out = out + jnp.dot((qd * jnp.exp(off)).astype(BF16).reshape(rows, kdim), s0.astype(BF16),
                        preferred_element_type=F32)
    upd = lax.dot_general((kd * jnp.exp(total - off_end)).astype(BF16).reshape(rows, kdim), vb,
                          (((0,), (0,)), ((), ())), preferred_element_type=F32)
    return out, s0 * _decay_columns(jnp.exp(total), vdim) + upd


def _decay_columns(e_row, vdim):
    kdim = e_row.shape[-1]
    e_col = jnp.transpose(jnp.broadcast_to(e_row, (LANES, kdim)))
    if vdim > LANES:
        e_col = jnp.concatenate([e_col] * (vdim // LANES), axis=1)
    return e_col


def _gated_out(o, g, nw):
    ms = jnp.mean(o * o, axis=-1, keepdims=True)
    return (o * lax.rsqrt(ms + EPS) * nw * _silu(g)).astype(BF16)


def _rec_kernel(*refs, mixer, fused, layer_j, has_s0, per_seq_state, gsz, valid, n_sb, n_tblocks):
    refs = list(refs)
    if fused:
        h_ref, w_ref, lb_ref, nw_ref = refs[:4]
        refs = refs[4:]
    elif mixer == "hgrn":
        a_ref, f_ref, v_ref, g_ref, lb_ref, nw_ref = refs[:6]
        refs = refs[6:]
    else:
        a_ref, k_ref, v_ref, g_ref, f_ref, nw_ref = refs[:6]
        refs = refs[6:]
    s0_ref = refs.pop(0) if has_s0 else None
    og_ref, so_ref = refs[:2]
    scr = refs[2:]
    kdim, vdim = so_ref.shape[-2:]
    ti = pl.program_id(2)

    if mixer == "hgrn":
        raw = lb_ref[...]
        ex = jnp.exp(raw - jnp.max(raw, axis=0, keepdims=True))
        p = ex / jnp.sum(ex, axis=0, keepdims=True)
        cum = p[0:1, :]
        for r in range(1, layer_j + 1):
            cum = cum + p[r:r + 1, :]
        lb = jnp.clip(cum - p[0:1, :], 0.0, 1.0 - 1e-6)
        log_lb = jnp.log(lb)
        log_1m = jnp.log1p(-lb)
        one_m = 1.0 - lb
    nw = nw_ref[...]

    def fields(load):
        if fused:
            proj = load(None)
            a, fp, v, g = (proj[..., i * kdim:(i + 1) * kdim] for i in range(4))
        else:
            a, v, g = load(a_ref), load(v_ref), load(g_ref)
        if mixer == "hgrn":
            if not fused:
                fp = load(f_ref)
            q = _silu(a)
            y = log_1m + _log_sigmoid(fp)
            lf = jnp.maximum(log_lb, y) + jnp.log1p(jnp.exp(-jnp.abs(log_lb - y)))
            k = one_m * _sigmoid(-fp)
        else:
            q = a * (kdim ** -0.5)
            k = load(k_ref)
            lf = load(f_ref)
        return q, k, v, lf, g

    if per_seq_state:
        pq, pk, pf, pv = scr
        for r in scr:
            r[...] = jnp.zeros(r.shape, F32)
        q, k, v, lf, g = fields(lambda r: r[...])
        pq[:, 0:valid, :] = q
        pk[:, 0:valid, :] = k
        pf[:, 0:valid, :] = lf
        pv[:, 0:valid, :] = v
        o, qd, kd, b_last = _chunk_parts(pq[...], pk[...], pv[...], pf[...], valid)
        qd = qd.astype(BF16)
        upd = jnp.einsum("gtk,gtv->gkv", kd.astype(BF16), pv[...].astype(BF16), preferred_element_type=F32)
        e_last = jnp.exp(b_last)
        outs = []
        for i in range(gsz):
            s = s0_ref[i]
            outs.append(o[i, 0:valid, :] + jnp.dot(qd[i], s.astype(BF16), preferred_element_type=F32)[0:valid, :])
            so_ref[i] = s * _decay_columns(e_last[i], vdim) + upd[i]
        og_ref[...] = _gated_out(jnp.stack(outs), g, nw)
        return

    s_scr = scr[0]

    @pl.when(ti == 0)
    def _():
        if has_s0:
            s_scr[...] = s0_ref[0]
        else:
            s_scr[...] = jnp.zeros(s_scr.shape, F32)

    rows_sb = gsz * CHUNK

    def time_block(src_ref):
        hierarchical = kdim * vdim > LANES * LANES
        masks = _level_masks(gsz, CHUNK) if hierarchical else None
        for sb in range(n_sb):
            rows = pl.ds(sb * rows_sb, rows_sb)

            def load(r):
                x = src_ref[rows, :] if r is None else r[0, rows, :]
                return x.reshape(gsz, CHUNK, x.shape[-1])

            q, k, v, lf, g = fields(load)
            if hierarchical:
                out, s_scr[...] = _superblock(q, k, v, lf, s_scr[...], masks)
            else:
                out, s_scr[...] = _superblock_chain(q, k, v, lf, s_scr[...])
            og_ref[0, rows, :] = _gated_out(out, g.reshape(rows_sb, vdim), nw)

    if not fused:
        time_block(None)

        @pl.when(ti == n_tblocks - 1)
        def _():
            so_ref[0] = s_scr[...]
        return

    proj_a, proj_b = scr[1:]

    def project(dst_ref):
        dst_ref[...] = jnp.dot(h_ref[0], w_ref[...], preferred_element_type=F32)

    @pl.when(ti == 0)
    def _():
        project(proj_a)

    odd = lax.rem(ti, 2) == 1

    @pl.when(odd)
    def _():
        project(proj_b)
        time_block(proj_a)

    @pl.when(jnp.logical_and(jnp.logical_not(odd), ti > 0))
    def _():
        project(proj_a)
        time_block(proj_b)

    @pl.when(ti == n_tblocks)
    def _():
        so_ref[0] = s_scr[...]


def _recurrence(mixer, layer_j, n_layers, proj3, lf3, lb_raw, norm_w, state, state_out, heads, kdim, vdim,
                bb, tb, gsz, w_heads=None):
    bsz, t, _ = proj3.shape
    has_s0 = state is not None
    per_seq_state = t < CHUNK
    fused = w_heads is not None
    n_tblocks = t // tb
    kb, vb = kdim, vdim
    nk = heads * kdim // kb
    nv = heads * vdim // vb

    def tok(width, base, stride=1):
        return pl.BlockSpec((bb, tb, width), lambda b, h, ti: (b, ti, base + stride * h))

    nw_spec = pl.BlockSpec((None, 1, vb), lambda b, h, ti: (layer_j, 0, h))
    nw3 = norm_w.reshape(norm_w.shape[0], 1, -1)
    lb_spec = None if lb_raw is None else pl.BlockSpec((lb_raw.shape[0], kb), lambda b, h, ti: (0, h))
    if fused:
        assert mixer == "hgrn" and kdim == vdim and not per_seq_state
        d = proj3.shape[-1]
        in_specs = [pl.BlockSpec((bb, tb, d), lambda b, h, ti: (b, jnp.minimum(ti, n_tblocks - 1), 0)),
                    pl.BlockSpec((None, None, d, 4 * kdim), lambda b, h, ti: (layer_j, h, 0, 0)), lb_spec, nw_spec]
        args = [proj3, w_heads, lb_raw, nw3]
    elif mixer == "hgrn":
        assert kdim == vdim
        in_specs = [tok(kb, 0, 4), tok(kb, 1, 4), tok(vb, 2, 4), tok(vb, 3, 4), lb_spec, nw_spec]
        args = [proj3, proj3, proj3, proj3, lb_raw, nw3]
    else:
        v_base = 2 * heads * kdim // vb
        in_specs = [tok(kb, 0), tok(kb, nk), tok(vb, v_base), tok(vb, v_base + nv), tok(kb, 0), nw_spec]
        args = [proj3, proj3, proj3, proj3, lf3, nw3]
    if has_s0:
        in_specs.append(pl.BlockSpec((None, bb, None, kdim, vdim), lambda b, h, ti: (layer_j, b, h, 0, 0)))
        args.append(state)
    aliases = {}
    if state_out is not None:
        in_specs.append(pl.BlockSpec(memory_space=pl.ANY))
        args.append(state_out)
        aliases = {len(args) - 1: 1}
    if per_seq_state:
        assert has_s0 and tb == t and gsz == bb
        scratch = [pltpu.VMEM((bb, CHUNK, kdim), F32)] * 3 + [pltpu.VMEM((bb, CHUNK, vdim), F32)]
        valid, n_sb = t, 1
    else:
        assert bb == 1 and tb % (gsz * CHUNK) == 0
        scratch = [pltpu.VMEM((kdim, vdim), F32)]
        if fused:
            scratch += [pltpu.VMEM((tb, 4 * kdim), F32)] * 2
        valid, n_sb = CHUNK, tb // (gsz * CHUNK)
    n_steps = n_tblocks + 1 if fused else n_tblocks
    og_block = (lambda b, h, ti: (b, jnp.maximum(ti - 1, 0), h)) if fused else (lambda b, h, ti: (b, ti, h))

    def kern(*refs):
        if state_out is not None:
            refs = refs[:len(args) - 1] + refs[len(args):]
        _rec_kernel(*refs, mixer=mixer, fused=fused, layer_j=layer_j, has_s0=has_s0, per_seq_state=per_seq_state,
                    gsz=gsz, valid=valid, n_sb=n_sb, n_tblocks=n_tblocks)

    return pl.pallas_call(
        kern,
        grid=(bsz // bb, heads, n_steps),
        in_specs=in_specs,
        out_specs=[pl.BlockSpec((bb, tb, vb), og_block),
                   pl.BlockSpec((None, bb, None, kdim, vdim), lambda b, h, ti: (layer_j, b, h, 0, 0))],
        out_shape=[jax.ShapeDtypeStruct((bsz, t, heads * vdim), BF16),
                   jax.ShapeDtypeStruct((n_layers, bsz, heads, kdim, vdim), F32)],
        scratch_shapes=scratch,
        input_output_aliases=aliases,
        compiler_params=_params(("arbitrary", "arbitrary", "arbitrary"), VMEM_LIMIT),
        name=mixer + "_rec",
    )(*args)


def _outproj_kernel(og_ref, w_ref, x_ref, g_ref, sc_ref, sh_ref, lnw_ref, lnb_ref, xo_ref, ho_ref, *, alpha):
    y = jnp.dot(og_ref[...], w_ref[...], preferred_element_type=F32)
    xn = _layer_norm(alpha * x_ref[...] + (1.0 + g_ref[...]) * y, lnw_ref[...], lnb_ref[...])
    xo_ref[...] = xn
    ho_ref[...] = (xn * (1.0 + sc_ref[...]) + sh_ref[...]).astype(BF16)


def _ln_specs(layer, sub, d, n_grid):
    idx = (lambda i: (layer * 2 + sub, 0, 0)) if n_grid == 1 else (lambda i, f: (layer * 2 + sub, 0, 0))
    return [pl.BlockSpec((None, 1, d), idx), pl.BlockSpec((None, 1, d), idx)]


def _outproj(grp, og, w_out, layer_j, x, layer, ln_w, ln_b, alpha):
    m, d = grp.m, grp.d
    sub = grp.with_tile(min(grp.tm, 256))
    tm = sub.tm
    din = w_out.shape[1]
    row = lambda i: (i, 0)
    return pl.pallas_call(
        functools.partial(_outproj_kernel, alpha=alpha),
        grid=(m // tm,),
        in_specs=[pl.BlockSpec((tm, din), row), pl.BlockSpec((None, din, d), lambda i: (layer_j, 0, 0)),
                  pl.BlockSpec((tm, d), row), sub.mod_spec(layer, 2), sub.mod_spec(layer, 4),
                  sub.mod_spec(layer, 3)] + _ln_specs(layer, 0, d, 1),
        out_specs=[pl.BlockSpec((tm, d), row), pl.BlockSpec((tm, d), row)],
        out_shape=[jax.ShapeDtypeStruct((m, d), F32), jax.ShapeDtypeStruct((m, d), BF16)],
        compiler_params=_params(("arbitrary",), VMEM_LIMIT),
        name="out_proj_ln",
    )(og, w_out, x, grp.mod, grp.mod, grp.mod, ln_w, ln_b)


def _mlp_kernel(*refs, alpha, n_f, emit_h):
    if emit_h:
        h_ref, wu_ref, wd_ref, x_ref, g_ref, sc_ref, sh_ref, lnw_ref, lnb_ref, xo_ref, ho_ref, acc_ref = refs
    else:
        h_ref, wu_ref, wd_ref, x_ref, g_ref, lnw_ref, lnb_ref, xo_ref, acc_ref = refs
    f = pl.program_id(1)

    @pl.when(f == 0)
    def _():
        acc_ref[...] = jnp.zeros(acc_ref.shape, F32)

    u = jnp.dot(h_ref[...], wu_ref[...], preferred_element_type=F32)
    u = jnp.square(jnp.maximum(u, 0.0)).astype(BF16)
    acc_ref[...] += jnp.dot(u, wd_ref[...], preferred_element_type=F32)

    @pl.when(f == n_f - 1)
    def _():
        xn = _layer_norm(alpha * x_ref[...] + (1.0 + g_ref[...]) * acc_ref[...], lnw_ref[...], lnb_ref[...])
        xo_ref[...] = xn
        if emit_h:
            ho_ref[...] = (xn * (1.0 + sc_ref[...]) + sh_ref[...]).astype(BF16)


def _mlp(grp, h, w_up, w_down, x, layer, ln_w, ln_b, alpha, emit_h):
    m, d, tm = grp.m, grp.d, grp.tm
    dff = w_up.shape[-1]
    tf = 1024
    n_f = dff // tf
    row = lambda i, f: (i, 0)
    in_specs = [pl.BlockSpec((tm, d), row), pl.BlockSpec((None, d, tf), lambda i, f: (layer, 0, f)),
                pl.BlockSpec((None, tf, d), lambda i, f: (layer, f, 0)), pl.BlockSpec((tm, d), row),
                grp.mod_spec(layer, 5)]
    args = [h, w_up, w_down, x, grp.mod]
    out_specs = [pl.BlockSpec((tm, d), row)]
    out_shape = [jax.ShapeDtypeStruct((m, d), F32)]
    if emit_h:
        in_specs += [grp.mod_spec(layer + 1, 1), grp.mod_spec(layer + 1, 0)]
        args += [grp.mod, grp.mod]
        out_specs.append(pl.BlockSpec((tm, d), row))
        out_shape.append(jax.ShapeDtypeStruct((m, d), BF16))
    in_specs += _ln_specs(layer, 1, d, 2)
    args += [ln_w, ln_b]
    outs = pl.pallas_call(
        functools.partial(_mlp_kernel, alpha=alpha, n_f=n_f, emit_h=emit_h),
        grid=(m // tm, n_f),
        in_specs=in_specs,
        out_specs=out_specs,
        out_shape=out_shape,
        scratch_shapes=[pltpu.VMEM((tm, d), F32)],
        compiler_params=_params(("arbitrary", "arbitrary"), VMEM_LIMIT),
        name="mlp_ln",
    )(*args)
    return (outs[0], outs[1]) if emit_h else (outs[0], None)


def _trunk(grp, x, s_hg, s_gla, w, rec_cfg):
    depth = w["w_up"].shape[0]
    alpha = (2.0 * depth) ** 0.25
    hg_heads, hg_k, hg_v = w["hg_dims"]
    gla_heads, gla_k, gla_v = w["gla_dims"]
    n_hg, n_gla = w["hg_w_out"].shape[0], w["gla_w_out"].shape[0]
    new_hg = new_gla = None
    h = _mod0(grp, x)
    for l in range(depth):
        j = l // 2
        if l % 2 == 0:
            if rec_cfg["hg_fused"]:
                src, w_heads = h, w["hg_w_heads"]
            else:
                src, w_heads = _inproj_heads(grp, h, w["hg_w_heads"], j), None
            og, new_hg = _recurrence("hgrn", j, n_hg, src.reshape(grp.batch, grp.seq, -1), None, w["hg_lb_raw"],
                                     w["hg_norm_w"], s_hg, new_hg, hg_heads, hg_k, hg_v,
                                     rec_cfg["hg_bb"], rec_cfg["tb"], rec_cfg["hg_gsz"], w_heads)
            w_out = w["hg_w_out"]
        else:
            proj = _inproj(grp, h, w["gla_w_in"], j, w["gla_main"])
            lf = _gla_gate(grp, h, w["gla_w_low"], w["gla_w_gk2"], w["gla_b_gk"], j)
            og, new_gla = _recurrence("gla", j, n_gla, proj.reshape(grp.batch, grp.seq, -1),
                                      lf.reshape(grp.batch, grp.seq, -1), None, w["gla_norm_w"], s_gla, new_gla,
                                      gla_heads, gla_k, gla_v, rec_cfg["gla_bb"], rec_cfg["tb"], rec_cfg["gla_gsz"])
            w_out = w["gla_w_out"]
        x, h2 = _outproj(grp, og.reshape(grp.m, -1), w_out, j, x, l, w["ln_w"], w["ln_b"], alpha)
        x, h = _mlp(grp, h2, w["w_up"], w["w_down"], x, l, w["ln_w"], w["ln_b"], alpha, emit_h=(l + 1 < depth))
    return x, new_hg, new_gla


def kernel(x_prompt, x_sample, state_hgrn, state_gla, c_prompt, c_sample, w_ada, b_ada, ln_w, ln_b,
           hg_w_in, hg_lb_raw, hg_norm_w, hg_w_out, gla_w_in, gla_w_gk2, gla_b_gk, gla_norm_w,
           gla_w_out, w_up, w_down):
    bp, tp, d = x_prompt.shape
    bs, ts, _ = x_sample.shape
    depth = w_up.shape[0]
    _, _, hg_heads, hg_k, hg_v = state_hgrn.shape
    _, _, gla_heads, gla_k, gla_v = state_gla.shape
    gla_main = 2 * gla_heads * gla_k + 2 * gla_heads * gla_v
    rank = gla_w_in.shape[-1] - gla_main

    ms = bs * ts
    pad_rows = (-(ms + bp)) % 8
    c_all = jnp.concatenate([jnp.repeat(c_sample, ts, axis=0), c_prompt, jnp.zeros((pad_rows, d), F32)], axis=0)
    mod = _ada(c_all, w_ada, b_ada)
    mod_p = mod[:, ms:ms + bp].reshape(depth, bp, 1, 6 * d)

    w = {
        "hg_w_heads": _regroup_heads(hg_w_in, hg_heads, 4), "hg_lb_raw": hg_lb_raw, "hg_norm_w": hg_norm_w,
        "hg_w_out": hg_w_out.astype(BF16),
        "gla_w_in": gla_w_in[..., :gla_main].astype(BF16), "gla_main": gla_main,
        "gla_w_low": jnp.pad(gla_w_in[..., gla_main:], ((0, 0), (0, 0), (0, LANES - rank))).astype(BF16),
        "gla_w_gk2": jnp.pad(gla_w_gk2, ((0, 0), (0, LANES - rank), (0, 0))).astype(BF16),
        "gla_b_gk": gla_b_gk, "gla_norm_w": gla_norm_w, "gla_w_out": gla_w_out.astype(BF16),
        "w_up": w_up.astype(BF16), "w_down": w_down.astype(BF16),
        "ln_w": ln_w.reshape(depth * 2, 1, d), "ln_b": ln_b.reshape(depth * 2, 1, d),
        "hg_dims": (hg_heads, hg_k, hg_v), "gla_dims": (gla_heads, gla_k, gla_v),
    }

    grp_p = _Group(bp, tp, d, 512, mod_p, per_token=False)
    grp_s = _Group(bs, ts, d, 512, mod, per_token=True)

    y_p, hg_p, gla_p = _trunk(grp_p, x_prompt.reshape(bp * tp, d), None, None, w,
                              {"hg_bb": 1, "gla_bb": 1, "tb": 512, "hg_gsz": 8, "gla_gsz": 8, "hg_fused": True})
    y_s, hg_s, gla_s = _trunk(grp_s, x_sample.reshape(ms, d), state_hgrn, state_gla, w,
                              {"hg_bb": 16, "gla_bb": 4, "tb": ts, "hg_gsz": 16, "gla_gsz": 4, "hg_fused": False})
    return (y_p.reshape(bp, tp, d), y_s.reshape(bs, ts, d), hg_p, gla_p, hg_s, gla_s)
```

```python
import functools

import jax
import jax.numpy as jnp
from jax import lax
from jax.experimental import pallas as pl
from jax.experimental.pallas import tpu as pltpu

F32 = jnp.float32
BF16 = jnp.bfloat16

EPS = 1e-5
LOG2E = 1.4426950408889634
GLA_GATE_NORM = 16.0
LANES = 128
SUBLANES = 8
CHUNK = 16
VMEM_LIMIT = 56 * 1024 * 1024


def _sigmoid(x):
    return 1.0 / (1.0 + jnp.exp(-x))


def _silu(x):
    return x * _sigmoid(x)


def _log_sigmoid(x):
    return jnp.minimum(x, 0.0) - jnp.log1p(jnp.exp(-jnp.abs(x)))


def _layer_norm(z, w, b):
    mu = jnp.mean(z, axis=-1, keepdims=True)
    zc = z - mu
    var = jnp.mean(zc * zc, axis=-1, keepdims=True)
    return zc * lax.rsqrt(var + EPS) * w + b


def _params(sem, vmem=None):
    return pltpu.CompilerParams(dimension_semantics=sem, vmem_limit_bytes=vmem)


def _ada_kernel(c_ref, w_ref, b_ref, o_ref):
    cs = _silu(c_ref[...]).astype(BF16)
    o_ref[...] = jnp.dot(cs, w_ref[...].astype(BF16), preferred_element_type=F32) + b_ref[...]


def _ada(c_all, w_ada, b_ada):
    depth, d, n6 = w_ada.shape
    rows = c_all.shape[0]
    tn = 1024
    return pl.pallas_call(
        _ada_kernel,
        grid=(depth, n6 // tn),
        in_specs=[pl.BlockSpec((rows, d), lambda l, j: (0, 0)),
                  pl.BlockSpec((None, d, tn), lambda l, j: (l, 0, j)),
                  pl.BlockSpec((None, 1, tn), lambda l, j: (l, 0, j))],
        out_specs=pl.BlockSpec((None, rows, tn), lambda l, j: (l, 0, j)),
        out_shape=jax.ShapeDtypeStruct((depth, rows, n6), F32),
        compiler_params=_params(("arbitrary", "arbitrary"), VMEM_LIMIT),
        name="ada_proj",
    )(c_all, w_ada, b_ada.reshape(depth, 1, n6))


class _Group:
    def __init__(self, batch, seq, d, tm, mod, per_token):
        self.batch, self.seq, self.d, self.tm = batch, seq, d, tm
        self.m = batch * seq
        self.mod = mod
        self.per_token = per_token

    def with_tile(self, tm):
        return _Group(self.batch, self.seq, self.d, tm, self.mod, self.per_token)

    def mod_spec(self, layer, chunk):
        d, tm = self.d, self.tm
        if self.per_token:
            return pl.BlockSpec((None, tm, d), lambda i, *_: (layer, i, chunk))
        tiles_per_b = self.seq // tm
        return pl.BlockSpec((None, None, 1, d), lambda i, *_: (layer, i // tiles_per_b, 0, chunk))


def _mod0_kernel(x_ref, sc_ref, sh_ref, h_ref):
    h_ref[...] = (x_ref[...] * (1.0 + sc_ref[...]) + sh_ref[...]).astype(BF16)


def _mod0(grp, x):
    m, d, tm = grp.m, grp.d, grp.tm
    return pl.pallas_call(
        _mod0_kernel,
        grid=(m // tm,),
        in_specs=[pl.BlockSpec((tm, d), lambda i: (i, 0)), grp.mod_spec(0, 1), grp.mod_spec(0, 0)],
        out_specs=pl.BlockSpec((tm, d), lambda i: (i, 0)),
        out_shape=jax.ShapeDtypeStruct((m, d), BF16),
        compiler_params=_params(("arbitrary",)),
        name="mod0",
    )(x, grp.mod, grp.mod)


def _mm_kernel(h_ref, w_ref, o_ref):
    o_ref[...] = jnp.dot(h_ref[...], w_ref[...], preferred_element_type=F32)


def _mm_cast_kernel(h_ref, w_ref, o_ref, wb_ref):
    @pl.when(pl.program_id(1) == 0)
    def _():
        wb_ref[...] = w_ref[...].astype(BF16)

    o_ref[...] = jnp.dot(h_ref[...], wb_ref[...], preferred_element_type=F32).astype(o_ref.dtype)


def _inproj(grp, h, w, layer, n, out_dtype):
    m, d, tm = grp.m, grp.d, grp.tm
    tn = 1024
    return pl.pallas_call(
        _mm_cast_kernel,
        grid=(n // tn, m // tm),
        in_specs=[pl.BlockSpec((tm, d), lambda j, i: (i, 0)),
                  pl.BlockSpec((None, d, tn), lambda j, i: (layer, 0, j))],
        out_specs=pl.BlockSpec((tm, tn), lambda j, i: (i, j)),
        out_shape=jax.ShapeDtypeStruct((m, n), out_dtype),
        scratch_shapes=[pltpu.VMEM((d, tn), BF16)],
        compiler_params=_params(("arbitrary", "arbitrary"), VMEM_LIMIT),
        name="in_proj",
    )(h, w)


def _regroup_kernel(*refs):
    *in_refs, o_ref = refs
    width = in_refs[0].shape[-1]
    for f, r in enumerate(in_refs):
        o_ref[:, f * width:(f + 1) * width] = r[...].astype(BF16)


def _regroup_heads(w, heads, n_fields):
    layers, d, n = w.shape
    kdim = n // (n_fields * heads)
    return pl.pallas_call(
        _regroup_kernel,
        grid=(layers, heads),
        in_specs=[pl.BlockSpec((None, d, kdim), functools.partial(lambda l, h, f: (l, 0, f * heads + h), f=f))
                  for f in range(n_fields)],
        out_specs=pl.BlockSpec((None, None, d, n_fields * kdim), lambda l, h: (l, h, 0, 0)),
        out_shape=jax.ShapeDtypeStruct((layers, heads, d, n_fields * kdim), BF16),
        compiler_params=_params(("arbitrary", "arbitrary")),
        name="regroup_heads",
    )(*([w] * n_fields))


def _inproj_heads(grp, h, w_heads, layer):
    m, d, tm = grp.m, grp.d, grp.tm
    _, heads, _, width = w_heads.shape
    return pl.pallas_call(
        _mm_kernel,
        grid=(heads, m // tm),
        in_specs=[pl.BlockSpec((tm, d), lambda j, i: (i, 0)),
                  pl.BlockSpec((None, None, d, width), lambda j, i: (layer, j, 0, 0))],
        out_specs=pl.BlockSpec((tm, width), lambda j, i: (i, j)),
        out_shape=jax.ShapeDtypeStruct((m, heads * width), F32),
        compiler_params=_params(("arbitrary", "arbitrary"), VMEM_LIMIT),
        name="in_proj_heads",
    )(h, w_heads)


def _gate_kernel(h_ref, wl_ref, w2_ref, b_ref, o_ref):
    low = jnp.dot(h_ref[...], wl_ref[...], preferred_element_type=F32)
    gk = jnp.dot(low.astype(BF16), w2_ref[...], preferred_element_type=F32) + b_ref[...]
    o_ref[...] = _log_sigmoid(gk) * (1.0 / GLA_GATE_NORM)


def _gla_gate(grp, h, w_low, w_gk2, b_gk, layer):
    m, d, tm = grp.m, grp.d, grp.tm
    kd = w_gk2.shape[-1]
    return pl.pallas_call(
        _gate_kernel,
        grid=(m // tm,),
        in_specs=[pl.BlockSpec((tm, d), lambda i: (i, 0)),
                  pl.BlockSpec((None, d, LANES), lambda i: (layer, 0, 0)),
                  pl.BlockSpec((None, LANES, kd), lambda i: (layer, 0, 0)),
                  pl.BlockSpec((None, 1, kd), lambda i: (layer, 0, 0))],
        out_specs=pl.BlockSpec((tm, kd), lambda i: (i, 0)),
        out_shape=jax.ShapeDtypeStruct((m, kd), F32),
        compiler_params=_params(("arbitrary",)),
        name="gla_gate",
    )(h, w_low, w_gk2, b_gk.reshape(-1, 1, kd))


def _chunk_cumsum(lf):
    g, c, kdim = lf.shape
    y = lf.reshape(g * c // SUBLANES, SUBLANES, kdim)
    row = lax.broadcasted_iota(jnp.int32, y.shape, 1)
    shift = 1
    while shift < SUBLANES:
        y = y + jnp.where(row >= shift, pltpu.roll(y, shift, 1), 0.0)
        shift *= 2
    y = y.reshape(g, c // SUBLANES, SUBLANES, kdim)
    parts = [y[:, 0]]
    for i in range(1, c // SUBLANES):
        parts.append(y[:, i] + parts[-1][:, SUBLANES - 1:SUBLANES, :])
    return jnp.stack(parts, axis=1).reshape(g, c, kdim)


def _chunk_parts(q, k, v, lf, valid):
    g, c, kdim = q.shape
    assert c == 2 * SUBLANES
    b = _chunk_cumsum(lf)
    g2 = 2 * g
    q8, k8, b8 = (x.reshape(g2, SUBLANES, kdim) for x in (q, k, b * LOG2E))
    col = lax.broadcasted_iota(jnp.int32, (g2, SUBLANES, c), 2)
    col0 = (lax.broadcasted_iota(jnp.int32, (g2, SUBLANES, c), 0) % 2) * SUBLANES
    att = jnp.zeros((g2, SUBLANES, c), F32)
    for j in range(min(valid, SUBLANES)):
        w = q8 * k8[:, j:j + 1, :] * jnp.exp2(jnp.minimum(b8 - b8[:, j:j + 1, :], 0.0))
        att = jnp.where(col == col0 + j, jnp.sum(w, axis=2, keepdims=True), att)
    att = jnp.where(col - col0 <= lax.broadcasted_iota(jnp.int32, (g2, SUBLANES, c), 1), att, 0.0)
    att = att.reshape(g, c, c)
    if valid > SUBLANES:
        r = b[:, SUBLANES - 1:SUBLANES, :]
        q_hi = (q * jnp.exp(jnp.minimum(b - r, 0.0))).astype(BF16)
        k_lo = (k * jnp.exp(jnp.minimum(r - b, 0.0))).astype(BF16)
        cross = jnp.einsum("gtk,gsk->gts", q_hi, k_lo, preferred_element_type=F32)
        row16 = lax.broadcasted_iota(jnp.int32, (g, c, c), 1)
        col16 = lax.broadcasted_iota(jnp.int32, (g, c, c), 2)
        att = jnp.where((row16 >= SUBLANES) & (col16 < SUBLANES), cross, att)
    o = jnp.einsum("gts,gsv->gtv", att.astype(BF16), v.astype(BF16), preferred_element_type=F32)
    b_last = b[:, c - 1:c, :]
    return o, q * jnp.exp(b), k * jnp.exp(b_last - b), b_last


def _superblock_chain(q, k, v, lf, s0):
    g, c, _ = q.shape
    vdim = v.shape[-1]
    o, qd, kd, b_last = _chunk_parts(q, k, v, lf, c)
    qd = qd.astype(BF16)
    upd = jnp.einsum("gtk,gtv->gkv", kd.astype(BF16), v.astype(BF16), preferred_element_type=F32)
    e_last = jnp.exp(b_last)
    s = s0
    outs = []
    for i in range(g):
        outs.append(o[i] + jnp.dot(qd[i], s.astype(BF16), preferred_element_type=F32))
        s = s * _decay_columns(e_last[i], vdim) + upd[i]
    return jnp.stack(outs).reshape(g * c, vdim), s


def _level_masks(g, c):
    rows = g * c
    row_i = lax.broadcasted_iota(jnp.int32, (rows, rows), 0)
    col_i = lax.broadcasted_iota(jnp.int32, (rows, rows), 1)
    masks = []
    bit = c
    while bit < rows:
        masks.append(((row_i ^ col_i) < 2 * bit) & ((row_i & bit) != 0) & ((col_i & bit) == 0))
        bit *= 2
    return masks


def _superblock(q, k, v, lf, s0, masks):
    g, c, kdim = q.shape
    vdim = v.shape[-1]
    rows = g * c
    o, qd, kd, b_last = _chunk_parts(q, k, v, lf, c)
    offs = [jnp.zeros((1, kdim), F32)]
    for i in range(1, g):
        offs.append(offs[-1] + b_last[i - 1])
    off = jnp.stack(offs)
    off_end = off + b_last
    total = off_end[g - 1]
    vb = v.astype(BF16).reshape(rows, vdim)
    att = jnp.zeros((rows, rows), F32)
    half = 1
    for take in masks:
        mid = jnp.stack([offs[(i // (2 * half)) * 2 * half + half] for i in range(g)])
        lhs = (qd * jnp.exp(jnp.minimum(off - mid, 0.0))).astype(BF16).reshape(rows, kdim)
        rhs = (kd * jnp.exp(jnp.minimum(mid - off_end, 0.0))).astype(BF16).reshape(rows, kdim)
        a = lax.dot_general(lhs, rhs, (((1,), (1,)), ((), ())), preferred_element_type=F32)
        att = jnp.where(take, a, att)
        half *= 2
    out = o.reshape(rows, vdim)
    if g > 1:
        out = out + jnp.dot(att.astype(BF16), vb, preferred_element_type=F32)
    out = out + jnp.dot((qd * jnp.exp(off)).astype(BF16).reshape(rows, kdim), s0.astype(BF16),
                        preferred_element_type=F32)
    upd = lax.dot_general((kd * jnp.exp(total - off_end)).astype(BF16).reshape(rows, kdim), vb,
                          (((0,), (0,)), ((), ())), preferred_element_type=F32)
    return out, s0 * _decay_columns(jnp.exp(total), vdim) + upd


def _decay_columns(e_row, vdim):
    kdim = e_row.shape[-1]
    e_col = jnp.transpose(jnp.broadcast_to(e_row, (LANES, kdim)))
    if vdim > LANES:
        e_col = jnp.concatenate([e_col] * (vdim // LANES), axis=1)
    return e_col


def _gated_out(o, g, nw):
    ms = jnp.mean(o * o, axis=-1, keepdims=True)
    return (o * lax.rsqrt(ms + EPS) * nw * _silu(g)).astype(BF16)


def _rec_kernel(*refs, mixer, fused, layer_j, has_s0, per_seq_state, gsz, valid, n_sb, n_tblocks):
    refs = list(refs)
    if fused:
        h_ref, w_ref, lb_ref, nw_ref = refs[:4]
        refs = refs[4:]
    elif mixer == "hgrn":
        a_ref, f_ref, v_ref, g_ref, lb_ref, nw_ref = refs[:6]
        refs = refs[6:]
    else:
        a_ref, k_ref, v_ref, g_ref, f_ref, nw_ref = refs[:6]
        refs = refs[6:]
    s0_ref = refs.pop(0) if has_s0 else None
    og_ref, so_ref = refs[:2]
    scr = refs[2:]
    kdim, vdim = so_ref.shape[-2:]
    ti = pl.program_id(2)

    if mixer == "hgrn":
        raw = lb_ref[...]
        ex = jnp.exp(raw - jnp.max(raw, axis=0, keepdims=True))
        p = ex / jnp.sum(ex, axis=0, keepdims=True)
        cum = p[0:1, :]
        for r in range(1, layer_j + 1):
            cum = cum + p[r:r + 1, :]
        lb = jnp.clip(cum - p[0:1, :], 0.0, 1.0 - 1e-6)
        log_lb = jnp.log(lb)
        log_1m = jnp.log1p(-lb)
        one_m = 1.0 - lb
    nw = nw_ref[...]

    def fields(load):
        if fused:
            proj = load(None)
            a, fp, v, g = (proj[..., i * kdim:(i + 1) * kdim] for i in range(4))
        else:
            a, v, g = load(a_ref), load(v_ref), load(g_ref)
        if mixer == "hgrn":
            if not fused:
                fp = load(f_ref)
            q = _silu(a)
            y = log_1m + _log_sigmoid(fp)
            lf = jnp.maximum(log_lb, y) + jnp.log1p(jnp.exp(-jnp.abs(log_lb - y)))
            k = one_m * _sigmoid(-fp)
        else:
            q = a * (kdim ** -0.5)
            k = load(k_ref)
            lf = load(f_ref)
        return q, k, v, lf, g

    if per_seq_state:
        pq, pk, pf, pv, po = scr
        for r in (pq, pk, pf, pv):
            r[...] = jnp.zeros(r.shape, F32)
        q, k, v, lf, g = fields(lambda r: r[...])
        for i in range(gsz):
            seq = slice(i * valid, (i + 1) * valid)
            pq[i, 0:valid, :] = q[seq]
            pk[i, 0:valid, :] = k[seq]
            pf[i, 0:valid, :] = lf[seq]
            pv[i, 0:valid, :] = v[seq]
        o, qd, kd, b_last = _chunk_parts(pq[...], pk[...], pv[...], pf[...], valid)
        qd = qd.astype(BF16)
        upd = jnp.einsum("gtk,gtv->gkv", kd.astype(BF16), pv[...].astype(BF16), preferred_element_type=F32)
        e_last = jnp.exp(b_last)
        for i in range(gsz):
            s = s0_ref[i]
            out = o[i] + jnp.dot(qd[i], s.astype(BF16), preferred_element_type=F32)
            po[i * valid:(i + 1) * valid, :] = out[0:valid]
            so_ref[i] = s * _decay_columns(e_last[i], vdim) + upd[i]
        og_ref[...] = _gated_out(po[...], g, nw)
        return

    s_scr = scr[0]

    @pl.when(ti == 0)
    def _():
        if has_s0:
            s_scr[...] = s0_ref[0]
        else:
            s_scr[...] = jnp.zeros(s_scr.shape, F32)

    rows_sb = gsz * CHUNK

    def time_block(src_ref):
        hierarchical = kdim * vdim > LANES * LANES
        masks = _level_masks(gsz, CHUNK) if hierarchical else None
        for sb in range(n_sb):
            rows = pl.ds(sb * rows_sb, rows_sb)

            def load(r):
                x = src_ref[rows, :] if r is None else r[0, rows, :].astype(F32)
                return x.reshape(gsz, CHUNK, x.shape[-1])

            q, k, v, lf, g = fields(load)
            if hierarchical:
                out, s_scr[...] = _superblock(q, k, v, lf, s_scr[...], masks)
            else:
                out, s_scr[...] = _superblock_chain(q, k, v, lf, s_scr[...])
            og_ref[0, rows, :] = _gated_out(out, g.reshape(rows_sb, vdim), nw)

    if not fused:
        time_block(None)

        @pl.when(ti == n_tblocks - 1)
        def _():
            so_ref[0] = s_scr[...]
        return

    proj_a, proj_b = scr[1:]

    def project(dst_ref):
        dst_ref[...] = jnp.dot(h_ref[0], w_ref[...], preferred_element_type=F32)

    @pl.when(ti == 0)
    def _():
        project(proj_a)

    odd = lax.rem(ti, 2) == 1

    @pl.when(odd)
    def _():
        project(proj_b)
        time_block(proj_a)

    @pl.when(jnp.logical_and(jnp.logical_not(odd), ti > 0))
    def _():
        project(proj_a)
        time_block(proj_b)

    @pl.when(ti == n_tblocks)
    def _():
        so_ref[0] = s_scr[...]


def _recurrence(mixer, layer_j, n_layers, proj3, lf3, lb_raw, norm_w, state, state_out, heads, kdim, vdim,
                bb, tb, gsz, w_heads=None):
    bsz, t, _ = proj3.shape
    has_s0 = state is not None
    per_seq_state = t < CHUNK
    fused = w_heads is not None
    n_tblocks = t // tb
    kb, vb = kdim, vdim
    nk = heads * kdim // kb
    nv = heads * vdim // vb

    if per_seq_state:
        proj3 = proj3.reshape(bsz * t, -1)
        lf3 = None if lf3 is None else lf3.reshape(bsz * t, -1)

    def tok(width, base, stride=1):
        if per_seq_state:
            return pl.BlockSpec((bb * t, width), lambda b, h, ti: (b, base + stride * h))
        return pl.BlockSpec((bb, tb, width), lambda b, h, ti: (b, ti, base + stride * h))

    nw_spec = pl.BlockSpec((None, 1, vb), lambda b, h, ti: (layer_j, 0, h))
    nw3 = norm_w.reshape(norm_w.shape[0], 1, -1)
    lb_spec = None if lb_raw is None else pl.BlockSpec((lb_raw.shape[0], kb), lambda b, h, ti: (0, h))
    if fused:
        assert mixer == "hgrn" and kdim == vdim and not per_seq_state
        d = proj3.shape[-1]
        in_specs = [pl.BlockSpec((bb, tb, d), lambda b, h, ti: (b, jnp.minimum(ti, n_tblocks - 1), 0)),
                    pl.BlockSpec((None, None, d, 4 * kdim), lambda b, h, ti: (layer_j, h, 0, 0)), lb_spec, nw_spec]
        args = [proj3, w_heads, lb_raw, nw3]
    elif mixer == "hgrn":
        assert kdim == vdim
        in_specs = [tok(kb, 0, 4), tok(kb, 1, 4), tok(vb, 2, 4), tok(vb, 3, 4), lb_spec, nw_spec]
        args = [proj3, proj3, proj3, proj3, lb_raw, nw3]
    else:
        v_base = 2 * heads * kdim // vb
        in_specs = [tok(kb, 0), tok(kb, nk), tok(vb, v_base), tok(vb, v_base + nv), tok(kb, 0), nw_spec]
        args = [proj3, proj3, proj3, proj3, lf3, nw3]
    if has_s0:
        in_specs.append(pl.BlockSpec((None, bb, None, kdim, vdim), lambda b, h, ti: (layer_j, b, h, 0, 0)))
        args.append(state)
    aliases = {}
    if state_out is not None:
        in_specs.append(pl.BlockSpec(memory_space=pl.ANY))
        args.append(state_out)
        aliases = {len(args) - 1: 1}
    if per_seq_state:
        assert has_s0 and tb == t and gsz == bb
        scratch = ([pltpu.VMEM((bb, CHUNK, kdim), F32)] * 3 + [pltpu.VMEM((bb, CHUNK, vdim), F32)]
                   + [pltpu.VMEM((bb * t, vdim), F32)])
        valid, n_sb = t, 1
    else:
        assert bb == 1 and tb % (gsz * CHUNK) == 0
        scratch = [pltpu.VMEM((kdim, vdim), F32)]
        if fused:
            scratch += [pltpu.VMEM((tb, 4 * kdim), F32)] * 2
        valid, n_sb = CHUNK, tb // (gsz * CHUNK)
    n_steps = n_tblocks + 1 if fused else n_tblocks
    if per_seq_state:
        og_spec = pl.BlockSpec((bb * t, vb), lambda b, h, ti: (b, h))
        og_shape = (bsz * t, heads * vdim)
    else:
        og_block = (lambda b, h, ti: (b, jnp.maximum(ti - 1, 0), h)) if fused else (lambda b, h, ti: (b, ti, h))
        og_spec = pl.BlockSpec((bb, tb, vb), og_block)
        og_shape = (bsz, t, heads * vdim)

    def kern(*refs):
        if state_out is not None:
            refs = refs[:len(args) - 1] + refs[len(args):]
        _rec_kernel(*refs, mixer=mixer, fused=fused, layer_j=layer_j, has_s0=has_s0, per_seq_state=per_seq_state,
                    gsz=gsz, valid=valid, n_sb=n_sb, n_tblocks=n_tblocks)

    return pl.pallas_call(
        kern,
        grid=(bsz // bb, heads, n_steps),
        in_specs=in_specs,
        out_specs=[og_spec,
                   pl.BlockSpec((None, bb, None, kdim, vdim), lambda b, h, ti: (layer_j, b, h, 0, 0))],
        out_shape=[jax.ShapeDtypeStruct(og_shape, BF16),
                   jax.ShapeDtypeStruct((n_layers, bsz, heads, kdim, vdim), F32)],
        scratch_shapes=scratch,
        input_output_aliases=aliases,
        compiler_params=_params(("arbitrary", "arbitrary", "arbitrary"), VMEM_LIMIT),
        name=mixer + "_rec",
    )(*args)


def _outproj_kernel(og_ref, w_ref, x_ref, g_ref, sc_ref, sh_ref, lnw_ref, lnb_ref, xo_ref, ho_ref, *, alpha):
    y = jnp.dot(og_ref[...], w_ref[...], preferred_element_type=F32)
    xn = _layer_norm(alpha * x_ref[...] + (1.0 + g_ref[...]) * y, lnw_ref[...], lnb_ref[...])
    xo_ref[...] = xn
    ho_ref[...] = (xn * (1.0 + sc_ref[...]) + sh_ref[...]).astype(BF16)


def _ln_specs(layer, sub, d, n_grid):
    idx = (lambda i: (layer * 2 + sub, 0, 0)) if n_grid == 1 else (lambda i, f: (layer * 2 + sub, 0, 0))
    return [pl.BlockSpec((None, 1, d), idx), pl.BlockSpec((None, 1, d), idx)]


def _outproj(grp, og, w_out, layer_j, x, layer, ln_w, ln_b, alpha):
    m, d = grp.m, grp.d
    sub = grp.with_tile(min(grp.tm, 256))
    tm = sub.tm
    din = w_out.shape[1]
    row = lambda i: (i, 0)
    return pl.pallas_call(
        functools.partial(_outproj_kernel, alpha=alpha),
        grid=(m // tm,),
        in_specs=[pl.BlockSpec((tm, din), row), pl.BlockSpec((None, din, d), lambda i: (layer_j, 0, 0)),
                  pl.BlockSpec((tm, d), row), sub.mod_spec(layer, 2), sub.mod_spec(layer, 4),
                  sub.mod_spec(layer, 3)] + _ln_specs(layer, 0, d, 1),
        out_specs=[pl.BlockSpec((tm, d), row), pl.BlockSpec((tm, d), row)],
        out_shape=[jax.ShapeDtypeStruct((m, d), F32), jax.ShapeDtypeStruct((m, d), BF16)],
        compiler_params=_params(("arbitrary",), VMEM_LIMIT),
        name="out_proj_ln",
    )(og, w_out, x, grp.mod, grp.mod, grp.mod, ln_w, ln_b)


def _mlp_kernel(*refs, alpha, n_f, emit_h):
    if emit_h:
        h_ref, wu_ref, wd_ref, x_ref, g_ref, sc_ref, sh_ref, lnw_ref, lnb_ref, xo_ref, ho_ref, acc_ref = refs
    else:
        h_ref, wu_ref, wd_ref, x_ref, g_ref, lnw_ref, lnb_ref, xo_ref, acc_ref = refs
    f = pl.program_id(1)

    @pl.when(f == 0)
    def _():
        acc_ref[...] = jnp.zeros(acc_ref.shape, F32)

    u = jnp.dot(h_ref[...], wu_ref[...], preferred_element_type=F32)
    u = jnp.square(jnp.maximum(u, 0.0)).astype(BF16)
    acc_ref[...] += jnp.dot(u, wd_ref[...], preferred_element_type=F32)

    @pl.when(f == n_f - 1)
    def _():
        xn = _layer_norm(alpha * x_ref[...] + (1.0 + g_ref[...]) * acc_ref[...], lnw_ref[...], lnb_ref[...])
        xo_ref[...] = xn
        if emit_h:
            ho_ref[...] = (xn * (1.0 + sc_ref[...]) + sh_ref[...]).astype(BF16)


def _mlp(grp, h, w_up, w_down, x, layer, ln_w, ln_b, alpha, emit_h):
    m, d, tm = grp.m, grp.d, grp.tm
    dff = w_up.shape[-1]
    tf = 1024
    n_f = dff // tf
    row = lambda i, f: (i, 0)
    in_specs = [pl.BlockSpec((tm, d), row), pl.BlockSpec((None, d, tf), lambda i, f: (layer, 0, f)),
                pl.BlockSpec((None, tf, d), lambda i, f: (layer, f, 0)), pl.BlockSpec((tm, d), row),
                grp.mod_spec(layer, 5)]
    args = [h, w_up, w_down, x, grp.mod]
    out_specs = [pl.BlockSpec((tm, d), row)]
    out_shape = [jax.ShapeDtypeStruct((m, d), F32)]
    if emit_h:
        in_specs += [grp.mod_spec(layer + 1, 1), grp.mod_spec(layer + 1, 0)]
        args += [grp.mod, grp.mod]
        out_specs.append(pl.BlockSpec((tm, d), row))
        out_shape.append(jax.ShapeDtypeStruct((m, d), BF16))
    in_specs += _ln_specs(layer, 1, d, 2)
    args += [ln_w, ln_b]
    outs = pl.pallas_call(
        functools.partial(_mlp_kernel, alpha=alpha, n_f=n_f, emit_h=emit_h),
        grid=(m // tm, n_f),
        in_specs=in_specs,
        out_specs=out_specs,
        out_shape=out_shape,
        scratch_shapes=[pltpu.VMEM((tm, d), F32)],
        compiler_params=_params(("arbitrary", "arbitrary"), VMEM_LIMIT),
        name="mlp_ln",
    )(*args)
    return (outs[0], outs[1]) if emit_h else (outs[0], None)


def _trunk(grp, x, s_hg, s_gla, w, rec_cfg):
    depth = w["w_up"].shape[0]
    alpha = (2.0 * depth) ** 0.25
    hg_heads, hg_k, hg_v = w["hg_dims"]
    gla_heads, gla_k, gla_v = w["gla_dims"]
    n_hg, n_gla = w["hg_w_out"].shape[0], w["gla_w_out"].shape[0]
    new_hg = new_gla = None
    h = _mod0(grp, x)
    for l in range(depth):
        j = l // 2
        if l % 2 == 0:
            if rec_cfg["hg_fused"]:
                src, w_heads = h, w["hg_w_heads"]
            else:
                src, w_heads = _inproj_heads(grp, h, w["hg_w_heads"], j), None
            og, new_hg = _recurrence("hgrn", j, n_hg, src.reshape(grp.batch, grp.seq, -1), None, w["hg_lb_raw"],
                                     w["hg_norm_w"], s_hg, new_hg, hg_heads, hg_k, hg_v,
                                     rec_cfg["hg_bb"], rec_cfg["tb"], rec_cfg["hg_gsz"], w_heads)
            w_out = w["hg_w_out"]
        else:
            proj = _inproj(grp, h, w["gla_w_in"], j, w["gla_main"], rec_cfg["proj_dtype"])
            lf = _gla_gate(grp, h, w["gla_w_low"], w["gla_w_gk2"], w["gla_b_gk"], j)
            og, new_gla = _recurrence("gla", j, n_gla, proj.reshape(grp.batch, grp.seq, -1),
                                      lf.reshape(grp.batch, grp.seq, -1), None, w["gla_norm_w"], s_gla, new_gla,
                                      gla_heads, gla_k, gla_v, rec_cfg["gla_bb"], rec_cfg["tb"], rec_cfg["gla_gsz"])
            w_out = w["gla_w_out"]
        x, h2 = _outproj(grp, og.reshape(grp.m, -1), w_out, j, x, l, w["ln_w"], w["ln_b"], alpha)
        x, h = _mlp(grp, h2, w["w_up"], w["w_down"], x, l, w["ln_w"], w["ln_b"], alpha, emit_h=(l + 1 < depth))
    return x, new_hg, new_gla


def kernel(x_prompt, x_sample, state_hgrn, state_gla, c_prompt, c_sample, w_ada, b_ada, ln_w, ln_b,
           hg_w_in, hg_lb_raw, hg_norm_w, hg_w_out, gla_w_in, gla_w_gk2, gla_b_gk, gla_norm_w,
           gla_w_out, w_up, w_down):
    bp, tp, d = x_prompt.shape
    bs, ts, _ = x_sample.shape
    depth = w_up.shape[0]
    _, _, hg_heads, hg_k, hg_v = state_hgrn.shape
    _, _, gla_heads, gla_k, gla_v = state_gla.shape
    gla_main = 2 * gla_heads * gla_k + 2 * gla_heads * gla_v
    rank = gla_w_in.shape[-1] - gla_main

    ms = bs * ts
    pad_rows = (-(ms + bp)) % 8
    c_all = jnp.concatenate([jnp.repeat(c_sample, ts, axis=0), c_prompt, jnp.zeros((pad_rows, d), F32)], axis=0)
    mod = _ada(c_all, w_ada, b_ada)
    mod_p = mod[:, ms:ms + bp].reshape(depth, bp, 1, 6 * d)

    w = {
        "hg_w_heads": _regroup_heads(hg_w_in, hg_heads, 4), "hg_lb_raw": hg_lb_raw, "hg_norm_w": hg_norm_w,
        "hg_w_out": hg_w_out.astype(BF16),
        "gla_w_in": gla_w_in, "gla_main": gla_main,
        "gla_w_low": jnp.pad(gla_w_in[..., gla_main:], ((0, 0), (0, 0), (0, LANES - rank))).astype(BF16),
        "gla_w_gk2": jnp.pad(gla_w_gk2, ((0, 0), (0, LANES - rank), (0, 0))).astype(BF16),
        "gla_b_gk": gla_b_gk, "gla_norm_w": gla_norm_w, "gla_w_out": gla_w_out.astype(BF16),
        "w_up": w_up.astype(BF16), "w_down": w_down.astype(BF16),
        "ln_w": ln_w.reshape(depth * 2, 1, d), "ln_b": ln_b.reshape(depth * 2, 1, d),
        "hg_dims": (hg_heads, hg_k, hg_v), "gla_dims": (gla_heads, gla_k, gla_v),
    }

    grp_p = _Group(bp, tp, d, 512, mod_p, per_token=False)
    grp_s = _Group(bs, ts, d, 512, mod, per_token=True)

    y_p, hg_p, gla_p = _trunk(grp_p, x_prompt.reshape(bp * tp, d), None, None, w,
                              {"hg_bb": 1, "gla_bb": 1, "tb": 512, "hg_gsz": 8, "gla_gsz": 8, "hg_fused": True,
                               "proj_dtype": BF16})
    y_s, hg_s, gla_s = _trunk(grp_s, x_sample.reshape(ms, d), state_hgrn, state_gla, w,
                              {"hg_bb": 32, "gla_bb": 8, "tb": ts, "hg_gsz": 32, "gla_gsz": 8, "hg_fused": False,
                               "proj_dtype": F32})
    return (y_p.reshape(bp, tp, d), y_s.reshape(bs, ts, d), hg_p, gla_p, hg_s, gla_s)
```

```python
import functools

import jax
import jax.numpy as jnp
from jax import lax
from jax.experimental import pallas as pl
from jax.experimental.pallas import tpu as pltpu

F32 = jnp.float32
BF16 = jnp.bfloat16

EPS = 1e-5
LOG2E = 1.4426950408889634
GLA_GATE_NORM = 16.0
LANES = 128
SUBLANES = 8
CHUNK = 16
VMEM_LIMIT = 56 * 1024 * 1024
PIECE_AFTER = (-1, 1)


def _sigmoid(x):
    return 0.5 * jnp.tanh(0.5 * x) + 0.5


def _silu(x):
    return x * _sigmoid(x)


def _log1p_exp_neg(d):
    return jnp.log(1.0 + jnp.exp(-d))


def _log_sigmoid(x):
    return jnp.minimum(x, 0.0) - _log1p_exp_neg(jnp.abs(x))


def _layer_norm(z, w, b):
    mu = jnp.mean(z, axis=-1, keepdims=True)
    zc = z - mu
    var = jnp.mean(zc * zc, axis=-1, keepdims=True)
    return zc * lax.rsqrt(var + EPS) * w + b


def _params(sem, vmem=None):
    return pltpu.CompilerParams(dimension_semantics=sem, vmem_limit_bytes=vmem)


def _ada_kernel(c_ref, w_ref, b_ref, o_ref):
    cs = _silu(c_ref[...]).astype(BF16)
    o_ref[...] = jnp.dot(cs, w_ref[...].astype(BF16), preferred_element_type=F32) + b_ref[...]


def _ada(c_all, w_ada, b_ada):
    depth, d, n6 = w_ada.shape
    rows = c_all.shape[0]
    tn = 1024
    return pl.pallas_call(
        _ada_kernel,
        grid=(depth, n6 // tn),
        in_specs=[pl.BlockSpec((rows, d), lambda l, j: (0, 0)),
                  pl.BlockSpec((None, d, tn), lambda l, j: (l, 0, j)),
                  pl.BlockSpec((None, 1, tn), lambda l, j: (l, 0, j))],
        out_specs=pl.BlockSpec((None, rows, tn), lambda l, j: (l, 0, j)),
        out_shape=jax.ShapeDtypeStruct((depth, rows, n6), F32),
        compiler_params=_params(("arbitrary", "arbitrary"), VMEM_LIMIT),
        name="ada_proj",
    )(c_all, w_ada, b_ada.reshape(depth, 1, n6))


class _Group:
    def __init__(self, batch, seq, d, tm, mod, per_token):
        self.batch, self.seq, self.d, self.tm = batch, seq, d, tm
        self.m = batch * seq
        self.mod = mod
        self.per_token = per_token

    def with_tile(self, tm):
        return _Group(self.batch, self.seq, self.d, tm, self.mod, self.per_token)

    def mod_spec(self, layer, chunk):
        d, tm = self.d, self.tm
        if self.per_token:
            return pl.BlockSpec((None, tm, d), lambda i, *_: (layer, i, chunk))
        tiles_per_b = self.seq // tm
        return pl.BlockSpec((None, None, 1, d), lambda i, *_: (layer, i // tiles_per_b, 0, chunk))


def _mod0_kernel(x_ref, sc_ref, sh_ref, h_ref):
    h_ref[...] = (x_ref[...] * (1.0 + sc_ref[...]) + sh_ref[...]).astype(BF16)


def _mod0(grp, x):
    m, d, tm = grp.m, grp.d, grp.tm
    return pl.pallas_call(
        _mod0_kernel,
        grid=(m // tm,),
        in_specs=[pl.BlockSpec((tm, d), lambda i: (i, 0)), grp.mod_spec(0, 1), grp.mod_spec(0, 0)],
        out_specs=pl.BlockSpec((tm, d), lambda i: (i, 0)),
        out_shape=jax.ShapeDtypeStruct((m, d), BF16),
        compiler_params=_params(("arbitrary",)),
        name="mod0",
    )(x, grp.mod, grp.mod)


def _mm_kernel(h_ref, w_ref, o_ref):
    o_ref[...] = jnp.dot(h_ref[...], w_ref[...], preferred_element_type=F32)


def _mm_cast_kernel(h_ref, w_ref, o_ref, wb_ref):
    @pl.when(pl.program_id(1) == 0)
    def _():
        wb_ref[...] = w_ref[...].astype(BF16)

    o_ref[...] = jnp.dot(h_ref[...], wb_ref[...], preferred_element_type=F32).astype(o_ref.dtype)


def _inproj(grp, h, w, layer, n, out_dtype):
    m, d, tm = grp.m, grp.d, grp.tm
    tn = 1024
    return pl.pallas_call(
        _mm_cast_kernel,
        grid=(n // tn, m // tm),
        in_specs=[pl.BlockSpec((tm, d), lambda j, i: (i, 0)),
                  pl.BlockSpec((None, d, tn), lambda j, i: (layer, 0, j))],
        out_specs=pl.BlockSpec((tm, tn), lambda j, i: (i, j)),
        out_shape=jax.ShapeDtypeStruct((m, n), out_dtype),
        scratch_shapes=[pltpu.VMEM((d, tn), BF16)],
        compiler_params=_params(("arbitrary", "arbitrary"), VMEM_LIMIT),
        name="in_proj",
    )(h, w)


def _regroup_kernel(*refs):
    *in_refs, o_ref = refs
    width = in_refs[0].shape[-1]
    for f, r in enumerate(in_refs):
        o_ref[:, f * width:(f + 1) * width] = r[...].astype(BF16)


def _regroup_heads(w, heads, n_fields):
    layers, d, n = w.shape
    kdim = n // (n_fields * heads)
    return pl.pallas_call(
        _regroup_kernel,
        grid=(layers, heads),
        in_specs=[pl.BlockSpec((None, d, kdim), functools.partial(lambda l, h, f: (l, 0, f * heads + h), f=f))
                  for f in range(n_fields)],
        out_specs=pl.BlockSpec((None, None, d, n_fields * kdim), lambda l, h: (l, h, 0, 0)),
        out_shape=jax.ShapeDtypeStruct((layers, heads, d, n_fields * kdim), BF16),
        compiler_params=_params(("arbitrary", "arbitrary")),
        name="regroup_heads",
    )(*([w] * n_fields))


def _inproj_heads(grp, h, w_heads, layer):
    m, d, tm = grp.m, grp.d, grp.tm
    _, heads, _, width = w_heads.shape
    return pl.pallas_call(
        _mm_kernel,
        grid=(heads, m // tm),
        in_specs=[pl.BlockSpec((tm, d), lambda j, i: (i, 0)),
                  pl.BlockSpec((None, None, d, width), lambda j, i: (layer, j, 0, 0))],
        out_specs=pl.BlockSpec((tm, width), lambda j, i: (i, j)),
        out_shape=jax.ShapeDtypeStruct((m, heads * width), F32),
        compiler_params=_params(("arbitrary", "arbitrary"), VMEM_LIMIT),
        name="in_proj_heads",
    )(h, w_heads)


def _gate_kernel(h_ref, wl_ref, w2_ref, b_ref, o_ref):
    low = jnp.dot(h_ref[...], wl_ref[...], preferred_element_type=F32)
    gk = jnp.dot(low.astype(BF16), w2_ref[...], preferred_element_type=F32) + b_ref[...]
    o_ref[...] = _log_sigmoid(gk) * (1.0 / GLA_GATE_NORM)


def _gla_gate(grp, h, w_low, w_gk2, b_gk, layer):
    m, d, tm = grp.m, grp.d, grp.tm
    kd = w_gk2.shape[-1]
    return pl.pallas_call(
        _gate_kernel,
        grid=(m // tm,),
        in_specs=[pl.BlockSpec((tm, d), lambda i: (i, 0)),
                  pl.BlockSpec((None, d, LANES), lambda i: (layer, 0, 0)),
                  pl.BlockSpec((None, LANES, kd), lambda i: (layer, 0, 0)),
                  pl.BlockSpec((None, 1, kd), lambda i: (layer, 0, 0))],
        out_specs=pl.BlockSpec((tm, kd), lambda i: (i, 0)),
        out_shape=jax.ShapeDtypeStruct((m, kd), F32),
        compiler_params=_params(("arbitrary",)),
        name="gla_gate",
    )(h, w_low, w_gk2, b_gk.reshape(-1, 1, kd))


def _chunk_cumsum(lf):
    g, c, kdim = lf.shape
    y = lf.reshape(g * c // SUBLANES, SUBLANES, kdim)
    row = lax.broadcasted_iota(jnp.int32, y.shape, 1)
    shift = 1
    while shift < SUBLANES:
        y = y + jnp.where(row >= shift, pltpu.roll(y, shift, 1), 0.0)
        shift *= 2
    y = y.reshape(g, c // SUBLANES, SUBLANES, kdim)
    parts = [y[:, 0]]
    for i in range(1, c // SUBLANES):
        parts.append(y[:, i] + parts[-1][:, SUBLANES - 1:SUBLANES, :])
    return jnp.stack(parts, axis=1).reshape(g, c, kdim)


def _chunk_parts(q, k, v, lf, valid):
    g, c, kdim = q.shape
    assert c == 2 * SUBLANES
    b = _chunk_cumsum(lf)
    g2 = 2 * g
    q8, k8, b8 = (x.reshape(g2, SUBLANES, kdim) for x in (q, k, b * LOG2E))
    col = lax.broadcasted_iota(jnp.int32, (g2, SUBLANES, c), 2)
    col0 = (lax.broadcasted_iota(jnp.int32, (g2, SUBLANES, c), 0) % 2) * SUBLANES
    att = jnp.zeros((g2, SUBLANES, c), F32)
    for j in range(min(valid, SUBLANES)):
        w = q8 * k8[:, j:j + 1, :] * jnp.exp2(jnp.minimum(b8 - b8[:, j:j + 1, :], 0.0))
        att = jnp.where(col == col0 + j, jnp.sum(w, axis=2, keepdims=True), att)
    att = jnp.where(col - col0 <= lax.broadcasted_iota(jnp.int32, (g2, SUBLANES, c), 1), att, 0.0)
    att = att.reshape(g, c, c)
    if valid > SUBLANES:
        r = b[:, SUBLANES - 1:SUBLANES, :]
        q_hi = (q * jnp.exp(jnp.minimum(b - r, 0.0))).astype(BF16)
        k_lo = (k * jnp.exp(jnp.minimum(r - b, 0.0))).astype(BF16)
        cross = jnp.einsum("gtk,gsk->gts", q_hi, k_lo, preferred_element_type=F32)
        row16 = lax.broadcasted_iota(jnp.int32, (g, c, c), 1)
        col16 = lax.broadcasted_iota(jnp.int32, (g, c, c), 2)
        att = jnp.where((row16 >= SUBLANES) & (col16 < SUBLANES), cross, att)
    o = jnp.einsum("gts,gsv->gtv", att.astype(BF16), v.astype(BF16), preferred_element_type=F32)
    b_last = b[:, c - 1:c, :]
    return o, q * jnp.exp(b), k * jnp.exp(b_last - b), b_last


def _superblock_chain(q, k, v, lf, s0):
    g, c, _ = q.shape
    vdim = v.shape[-1]
    o, qd, kd, b_last = _chunk_parts(q, k, v, lf, c)
    qd = qd.astype(BF16)
    upd = jnp.einsum("gtk,gtv->gkv", kd.astype(BF16), v.astype(BF16), preferred_element_type=F32)
    e_last = jnp.exp(b_last)
    s = s0
    outs = []
    for i in range(g):
        outs.append(o[i] + jnp.dot(qd[i], s.astype(BF16), preferred_element_type=F32))
        s = s * _decay_columns(e_last[i], vdim) + upd[i]
    return jnp.stack(outs).reshape(g * c, vdim), s


def _level_masks(g, c):
    rows = g * c
    row_i = lax.broadcasted_iota(jnp.int32, (rows, rows), 0)
    col_i = lax.broadcasted_iota(jnp.int32, (rows, rows), 1)
    masks = []
    bit = c
    while bit < rows:
        masks.append(((row_i ^ col_i) < 2 * bit) & ((row_i & bit) != 0) & ((col_i & bit) == 0))
        bit *= 2
    return masks


def _superblock(q, k, v, lf, s0, masks):
    g, c, kdim = q.shape
    vdim = v.shape[-1]
    rows = g * c
    o, qd, kd, b_last = _chunk_parts(q, k, v, lf, c)
    offs = [jnp.zeros((1, kdim), F32)]
    for i in range(1, g):
        offs.append(offs[-1] + b_last[i - 1])
    off = jnp.stack(offs)
    off_end = off + b_last
    total = off_end[g - 1]
    vb = v.astype(BF16).reshape(rows, vdim)
    att = jnp.zeros((rows, rows), F32)
    half = 1
    for take in masks:
        mid = jnp.stack([offs[(i // (2 * half)) * 2 * half + half] for i in range(g)])
        lhs = (qd * jnp.exp(jnp.minimum(off - mid, 0.0))).astype(BF16).reshape(rows, kdim)
        rhs = (kd * jnp.exp(jnp.minimum(mid - off_end, 0.0))).astype(BF16).reshape(rows, kdim)
        a = lax.dot_general(lhs, rhs, (((1,), (1,)), ((), ())), preferred_element_type=F32)
        att = jnp.where(take, a, att)
        half *= 2
    out = o.reshape(rows, vdim)
    if g > 1:
        out = out + jnp.dot(att.astype(BF16), vb, preferred_element_type=F32)
    out = out + jnp.dot((qd * jnp.exp(off)).astype(BF16).reshape(rows, kdim), s0.astype(BF16),
                        preferred_element_type=F32)
    upd = lax.dot_general((kd * jnp.exp(total - off_end)).astype(BF16).reshape(rows, kdim), vb,
                          (((0,), (0,)), ((), ())), preferred_element_type=F32)
    return out, s0 * _decay_columns(jnp.exp(total), vdim) + upd


def _decay_columns(e_row, vdim):
    kdim = e_row.shape[-1]
    e_col = jnp.transpose(jnp.broadcast_to(e_row, (LANES, kdim)))
    if vdim > LANES:
        e_col = jnp.concatenate([e_col] * (vdim // LANES), axis=1)
    return e_col


def _gated_out(o, g, nw):
    ms = jnp.mean(o * o, axis=-1, keepdims=True)
    return (o * lax.rsqrt(ms + EPS) * nw * _silu(g)).astype(BF16)


def _rec_kernel(*refs, mixer, fused, layer_j, has_s0, per_seq_state, gsz, valid, n_sb, n_tblocks):
    refs = list(refs)
    if fused:
        h_ref, w_ref, lb_ref, nw_ref = refs[:4]
        refs = refs[4:]
    elif mixer == "hgrn":
        a_ref, f_ref, v_ref, g_ref, lb_ref, nw_ref = refs[:6]
        refs = refs[6:]
    else:
        a_ref, k_ref, v_ref, g_ref, f_ref, nw_ref = refs[:6]
        refs = refs[6:]
    s0_ref = refs.pop(0) if has_s0 else None
    og_ref, so_ref = refs[:2]
    scr = refs[2:]
    kdim, vdim = so_ref.shape[-2:]
    ti = pl.program_id(2)

    if mixer == "hgrn":
        raw = lb_ref[...]
        ex = jnp.exp(raw - jnp.max(raw, axis=0, keepdims=True))
        p = ex / jnp.sum(ex, axis=0, keepdims=True)
        cum = p[0:1, :]
        for r in range(1, layer_j + 1):
            cum = cum + p[r:r + 1, :]
        lb = jnp.clip(cum - p[0:1, :], 0.0, 1.0 - 1e-6)
        log_lb = jnp.log(lb)
        log_1m = jnp.log1p(-lb)
        one_m = 1.0 - lb
    nw = nw_ref[...]

    def fields(load):
        if fused:
            proj = load(None)
            a, fp, v, g = (proj[..., i * kdim:(i + 1) * kdim] for i in range(4))
        else:
            a, v, g = load(a_ref), load(v_ref), load(g_ref)
        if mixer == "hgrn":
            if not fused:
                fp = load(f_ref)
            q = _silu(a)
            y = log_1m + _log_sigmoid(fp)
            lf = jnp.maximum(log_lb, y) + _log1p_exp_neg(jnp.abs(log_lb - y))
            k = one_m * _sigmoid(-fp)
        else:
            q = a * (kdim ** -0.5)
            k = load(k_ref)
            lf = load(f_ref)
        return q, k, v, lf, g

    if per_seq_state:
        pq, pk, pf, pv, po = scr
        for r in (pq, pk, pf, pv):
            r[...] = jnp.zeros(r.shape, F32)
        q, k, v, lf, g = fields(lambda r: r[...])
        for i in range(gsz):
            seq = slice(i * valid, (i + 1) * valid)
            pq[i, 0:valid, :] = q[seq]
            pk[i, 0:valid, :] = k[seq]
            pf[i, 0:valid, :] = lf[seq]
            pv[i, 0:valid, :] = v[seq]
        o, qd, kd, b_last = _chunk_parts(pq[...], pk[...], pv[...], pf[...], valid)
        qd = qd.astype(BF16)
        upd = jnp.einsum("gtk,gtv->gkv", kd.astype(BF16), pv[...].astype(BF16), preferred_element_type=F32)
        e_last = jnp.exp(b_last)
        for i in range(gsz):
            s = s0_ref[i]
            out = o[i] + jnp.dot(qd[i], s.astype(BF16), preferred_element_type=F32)
            po[i * valid:(i + 1) * valid, :] = out[0:valid]
            so_ref[i] = s * _decay_columns(e_last[i], vdim) + upd[i]
        og_ref[...] = _gated_out(po[...], g, nw)
        return

    s_scr = scr[0]

    @pl.when(ti == 0)
    def _():
        if has_s0:
            s_scr[...] = s0_ref[0]
        else:
            s_scr[...] = jnp.zeros(s_scr.shape, F32)

    rows_sb = gsz * CHUNK

    def time_block(src_ref, after=None):
        hierarchical = kdim * vdim > LANES * LANES
        masks = _level_masks(gsz, CHUNK) if hierarchical else None
        if after and -1 in after:
            after[-1]()
        for sb in range(n_sb):
            rows = pl.ds(sb * rows_sb, rows_sb)

            def load(r):
                x = src_ref[rows, :] if r is None else r[0, rows, :].astype(F32)
                return x.reshape(gsz, CHUNK, x.shape[-1])

            q, k, v, lf, g = fields(load)
            if hierarchical:
                out, s_scr[...] = _superblock(q, k, v, lf, s_scr[...], masks)
            else:
                out, s_scr[...] = _superblock_chain(q, k, v, lf, s_scr[...])
            og_ref[0, rows, :] = _gated_out(out, g.reshape(rows_sb, vdim), nw)
            if after and sb in after:
                after[sb]()

    if not fused:
        time_block(None)

        @pl.when(ti == n_tblocks - 1)
        def _():
            so_ref[0] = s_scr[...]
        return

    proj_a, proj_b = scr[1:]

    def project(dst_ref):
        dst_ref[...] = jnp.dot(h_ref[0], w_ref[...], preferred_element_type=F32)

    @pl.when(ti == 0)
    def _():
        project(proj_a)

    odd = lax.rem(ti, 2) == 1

    def project_cols(dst_ref, piece, n_pieces):
        width = dst_ref.shape[-1] // n_pieces
        cols = slice(piece * width, (piece + 1) * width)
        dst_ref[:, cols] = jnp.dot(h_ref[0], w_ref[:, cols], preferred_element_type=F32)

    @pl.when(odd)
    def _():
        time_block(proj_a, {sb: functools.partial(project_cols, proj_b, i, 2) for i, sb in enumerate(PIECE_AFTER)})

    @pl.when(jnp.logical_and(jnp.logical_not(odd), ti > 0))
    def _():
        time_block(proj_b, {sb: functools.partial(project_cols, proj_a, i, 2) for i, sb in enumerate(PIECE_AFTER)})

    @pl.when(ti == n_tblocks)
    def _():
        so_ref[0] = s_scr[...]


def _recurrence(mixer, layer_j, n_layers, proj3, lf3, lb_raw, norm_w, state, state_out, heads, kdim, vdim,
                bb, tb, gsz, w_heads=None):
    bsz, t, _ = proj3.shape
    has_s0 = state is not None
    per_seq_state = t < CHUNK
    fused = w_heads is not None
    n_tblocks = t // tb
    kb, vb = kdim, vdim
    nk = heads * kdim // kb
    nv = heads * vdim // vb

    if per_seq_state:
        proj3 = proj3.reshape(bsz * t, -1)
        lf3 = None if lf3 is None else lf3.reshape(bsz * t, -1)

    def tok(width, base, stride=1):
        if per_seq_state:
            return pl.BlockSpec((bb * t, width), lambda b, h, ti: (b, base + stride * h))
        return pl.BlockSpec((bb, tb, width), lambda b, h, ti: (b, ti, base + stride * h))

    nw_spec = pl.BlockSpec((None, 1, vb), lambda b, h, ti: (layer_j, 0, h))
    nw3 = norm_w.reshape(norm_w.shape[0], 1, -1)
    lb_spec = None if lb_raw is None else pl.BlockSpec((lb_raw.shape[0], kb), lambda b, h, ti: (0, h))
    if fused:
        assert mixer == "hgrn" and kdim == vdim and not per_seq_state
        d = proj3.shape[-1]
        in_specs = [pl.BlockSpec((bb, tb, d), lambda b, h, ti: (b, jnp.minimum(ti, n_tblocks - 1), 0)),
                    pl.BlockSpec((None, None, d, 4 * kdim), lambda b, h, ti: (layer_j, h, 0, 0)), lb_spec, nw_spec]
        args = [proj3, w_heads, lb_raw, nw3]
    elif mixer == "hgrn":
        assert kdim == vdim
        in_specs = [tok(kb, 0, 4), tok(kb, 1, 4), tok(vb, 2, 4), tok(vb, 3, 4), lb_spec, nw_spec]
        args = [proj3, proj3, proj3, proj3, lb_raw, nw3]
    else:
        v_base = 2 * heads * kdim // vb
        in_specs = [tok(kb, 0), tok(kb, nk), tok(vb, v_base), tok(vb, v_base + nv), tok(kb, 0), nw_spec]
        args = [proj3, proj3, proj3, proj3, lf3, nw3]
    if has_s0:
        in_specs.append(pl.BlockSpec((None, bb, None, kdim, vdim), lambda b, h, ti: (layer_j, b, h, 0, 0)))
        args.append(state)
    aliases = {}
    if state_out is not None:
        in_specs.append(pl.BlockSpec(memory_space=pl.ANY))
        args.append(state_out)
        aliases = {len(args) - 1: 1}
    if per_seq_state:
        assert has_s0 and tb == t and gsz == bb
        scratch = ([pltpu.VMEM((bb, CHUNK, kdim), F32)] * 3 + [pltpu.VMEM((bb, CHUNK, vdim), F32)]
                   + [pltpu.VMEM((bb * t, vdim), F32)])
        valid, n_sb = t, 1
    else:
        assert bb == 1 and tb % (gsz * CHUNK) == 0
        scratch = [pltpu.VMEM((kdim, vdim), F32)]
        if fused:
            scratch += [pltpu.VMEM((tb, 4 * kdim), F32)] * 2
        valid, n_sb = CHUNK, tb // (gsz * CHUNK)
    n_steps = n_tblocks + 1 if fused else n_tblocks
    if per_seq_state:
        og_spec = pl.BlockSpec((bb * t, vb), lambda b, h, ti: (b, h))
        og_shape = (bsz * t, heads * vdim)
    else:
        og_block = (lambda b, h, ti: (b, jnp.maximum(ti - 1, 0), h)) if fused else (lambda b, h, ti: (b, ti, h))
        og_spec = pl.BlockSpec((bb, tb, vb), og_block)
        og_shape = (bsz, t, heads * vdim)

    def kern(*refs):
        if state_out is not None:
            refs = refs[:len(args) - 1] + refs[len(args):]
        _rec_kernel(*refs, mixer=mixer, fused=fused, layer_j=layer_j, has_s0=has_s0, per_seq_state=per_seq_state,
                    gsz=gsz, valid=valid, n_sb=n_sb, n_tblocks=n_tblocks)

    return pl.pallas_call(
        kern,
        grid=(bsz // bb, heads, n_steps),
        in_specs=in_specs,
        out_specs=[og_spec,
                   pl.BlockSpec((None, bb, None, kdim, vdim), lambda b, h, ti: (layer_j, b, h, 0, 0))],
        out_shape=[jax.ShapeDtypeStruct(og_shape, BF16),
                   jax.ShapeDtypeStruct((n_layers, bsz, heads, kdim, vdim), F32)],
        scratch_shapes=scratch,
        input_output_aliases=aliases,
        compiler_params=_params(("arbitrary", "arbitrary", "arbitrary"), VMEM_LIMIT),
        name=mixer + "_rec",
    )(*args)


def _outproj_kernel(og_ref, w_ref, x_ref, g_ref, sc_ref, sh_ref, lnw_ref, lnb_ref, xo_ref, ho_ref, *, alpha):
    y = jnp.dot(og_ref[...], w_ref[...], preferred_element_type=F32)
    xn = _layer_norm(alpha * x_ref[...] + (1.0 + g_ref[...]) * y, lnw_ref[...], lnb_ref[...])
    xo_ref[...] = xn
    ho_ref[...] = (xn * (1.0 + sc_ref[...]) + sh_ref[...]).astype(BF16)


def _ln_specs(layer, sub, d, n_grid):
    idx = (lambda i: (layer * 2 + sub, 0, 0)) if n_grid == 1 else (lambda i, f: (layer * 2 + sub, 0, 0))
    return [pl.BlockSpec((None, 1, d), idx), pl.BlockSpec((None, 1, d), idx)]


def _outproj(grp, og, w_out, layer_j, x, layer, ln_w, ln_b, alpha):
    m, d = grp.m, grp.d
    sub = grp.with_tile(min(grp.tm, 256))
    tm = sub.tm
    din = w_out.shape[1]
    row = lambda i: (i, 0)
    return pl.pallas_call(
        functools.partial(_outproj_kernel, alpha=alpha),
        grid=(m // tm,),
        in_specs=[pl.BlockSpec((tm, din), row), pl.BlockSpec((None, din, d), lambda i: (layer_j, 0, 0)),
                  pl.BlockSpec((tm, d), row), sub.mod_spec(layer, 2), sub.mod_spec(layer, 4),
                  sub.mod_spec(layer, 3)] + _ln_specs(layer, 0, d, 1),
        out_specs=[pl.BlockSpec((tm, d), row), pl.BlockSpec((tm, d), row)],
        out_shape=[jax.ShapeDtypeStruct((m, d), F32), jax.ShapeDtypeStruct((m, d), BF16)],
        compiler_params=_params(("arbitrary",), VMEM_LIMIT),
        name="out_proj_ln",
    )(og, w_out, x, grp.mod, grp.mod, grp.mod, ln_w, ln_b)


def _mlp_kernel(*refs, alpha, n_f, emit_h):
    if emit_h:
        h_ref, wu_ref, wd_ref, x_ref, g_ref, sc_ref, sh_ref, lnw_ref, lnb_ref, xo_ref, ho_ref, acc_ref = refs
    else:
        h_ref, wu_ref, wd_ref, x_ref, g_ref, lnw_ref, lnb_ref, xo_ref, acc_ref = refs
    f = pl.program_id(1)

    @pl.when(f == 0)
    def _():
        acc_ref[...] = jnp.zeros(acc_ref.shape, F32)

    u = jnp.dot(h_ref[...], wu_ref[...], preferred_element_type=F32)
    u = jnp.square(jnp.maximum(u, 0.0)).astype(BF16)
    acc_ref[...] += jnp.dot(u, wd_ref[...], preferred_element_type=F32)

    @pl.when(f == n_f - 1)
    def _():
        xn = _layer_norm(alpha * x_ref[...] + (1.0 + g_ref[...]) * acc_ref[...], lnw_ref[...], lnb_ref[...])
        xo_ref[...] = xn
        if emit_h:
            ho_ref[...] = (xn * (1.0 + sc_ref[...]) + sh_ref[...]).astype(BF16)


def _mlp(grp, h, w_up, w_down, x, layer, ln_w, ln_b, alpha, emit_h):
    m, d, tm = grp.m, grp.d, grp.tm
    dff = w_up.shape[-1]
    tf = 1024
    n_f = dff // tf
    row = lambda i, f: (i, 0)
    in_specs = [pl.BlockSpec((tm, d), row), pl.BlockSpec((None, d, tf), lambda i, f: (layer, 0, f)),
                pl.BlockSpec((None, tf, d), lambda i, f: (layer, f, 0)), pl.BlockSpec((tm, d), row),
                grp.mod_spec(layer, 5)]
    args = [h, w_up, w_down, x, grp.mod]
    out_specs = [pl.BlockSpec((tm, d), row)]
    out_shape = [jax.ShapeDtypeStruct((m, d), F32)]
    if emit_h:
        in_specs += [grp.mod_spec(layer + 1, 1), grp.mod_spec(layer + 1, 0)]
        args += [grp.mod, grp.mod]
        out_specs.append(pl.BlockSpec((tm, d), row))
        out_shape.append(jax.ShapeDtypeStruct((m, d), BF16))
    in_specs += _ln_specs(layer, 1, d, 2)
    args += [ln_w, ln_b]
    outs = pl.pallas_call(
        functools.partial(_mlp_kernel, alpha=alpha, n_f=n_f, emit_h=emit_h),
        grid=(m // tm, n_f),
        in_specs=in_specs,
        out_specs=out_specs,
        out_shape=out_shape,
        scratch_shapes=[pltpu.VMEM((tm, d), F32)],
        compiler_params=_params(("arbitrary", "arbitrary"), VMEM_LIMIT),
        name="mlp_ln",
    )(*args)
    return (outs[0], outs[1]) if emit_h else (outs[0], None)


def _trunk(grp, x, s_hg, s_gla, w, rec_cfg):
    depth = w["w_up"].shape[0]
    alpha = (2.0 * depth) ** 0.25
    hg_heads, hg_k, hg_v = w["hg_dims"]
    gla_heads, gla_k, gla_v = w["gla_dims"]
    n_hg, n_gla = w["hg_w_out"].shape[0], w["gla_w_out"].shape[0]
    new_hg = new_gla = None
    h = _mod0(grp, x)
    for l in range(depth):
        j = l // 2
        if l % 2 == 0:
            if rec_cfg["hg_fused"]:
                src, w_heads = h, w["hg_w_heads"]
            else:
                src, w_heads = _inproj_heads(grp, h, w["hg_w_heads"], j), None
            og, new_hg = _recurrence("hgrn", j, n_hg, src.reshape(grp.batch, grp.seq, -1), None, w["hg_lb_raw"],
                                     w["hg_norm_w"], s_hg, new_hg, hg_heads, hg_k, hg_v,
                                     rec_cfg["hg_bb"], rec_cfg["tb"], rec_cfg["hg_gsz"], w_heads)
            w_out = w["hg_w_out"]
        else:
            proj = _inproj(grp, h, w["gla_w_in"], j, w["gla_main"], rec_cfg["proj_dtype"])
            lf = _gla_gate(grp, h, w["gla_w_low"], w["gla_w_gk2"], w["gla_b_gk"], j)
            og, new_gla = _recurrence("gla", j, n_gla, proj.reshape(grp.batch, grp.seq, -1),
                                      lf.reshape(grp.batch, grp.seq, -1), None, w["gla_norm_w"], s_gla, new_gla,
                                      gla_heads, gla_k, gla_v, rec_cfg["gla_bb"], rec_cfg["tb"], rec_cfg["gla_gsz"])
            w_out = w["gla_w_out"]
        x, h2 = _outproj(grp, og.reshape(grp.m, -1), w_out, j, x, l, w["ln_w"], w["ln_b"], alpha)
        x, h = _mlp(grp, h2, w["w_up"], w["w_down"], x, l, w["ln_w"], w["ln_b"], alpha, emit_h=(l + 1 < depth))
    return x, new_hg, new_gla


def kernel(x_prompt, x_sample, state_hgrn, state_gla, c_prompt, c_sample, w_ada, b_ada, ln_w, ln_b,
           hg_w_in, hg_lb_raw, hg_norm_w, hg_w_out, gla_w_in, gla_w_gk2, gla_b_gk, gla_norm_w,
           gla_w_out, w_up, w_down):
    bp, tp, d = x_prompt.shape
    bs, ts, _ = x_sample.shape
    depth = w_up.shape[0]
    _, _, hg_heads, hg_k, hg_v = state_hgrn.shape
    _, _, gla_heads, gla_k, gla_v = state_gla.shape
    gla_main = 2 * gla_heads * gla_k + 2 * gla_heads * gla_v
    rank = gla_w_in.shape[-1] - gla_main

    ms = bs * ts
    pad_rows = (-(ms + bp)) % 8
    c_all = jnp.concatenate([jnp.repeat(c_sample, ts, axis=0), c_prompt, jnp.zeros((pad_rows, d), F32)], axis=0)
    mod = _ada(c_all, w_ada, b_ada)
    mod_p = mod[:, ms:ms + bp].reshape(depth, bp, 1, 6 * d)

    w = {
        "hg_w_heads": _regroup_heads(hg_w_in, hg_heads, 4), "hg_lb_raw": hg_lb_raw, "hg_norm_w": hg_norm_w,
        "hg_w_out": hg_w_out.astype(BF16),
        "gla_w_in": gla_w_in, "gla_main": gla_main,
        "gla_w_low": jnp.pad(gla_w_in[..., gla_main:], ((0, 0), (0, 0), (0, LANES - rank))).astype(BF16),
        "gla_w_gk2": jnp.pad(gla_w_gk2, ((0, 0), (0, LANES - rank), (0, 0))).astype(BF16),
        "gla_b_gk": gla_b_gk, "gla_norm_w": gla_norm_w, "gla_w_out": gla_w_out.astype(BF16),
        "w_up": w_up.astype(BF16), "w_down": w_down.astype(BF16),
        "ln_w": ln_w.reshape(depth * 2, 1, d), "ln_b": ln_b.reshape(depth * 2, 1, d),
        "hg_dims": (hg_heads, hg_k, hg_v), "gla_dims": (gla_heads, gla_k, gla_v),
    }

    grp_p = _Group(bp, tp, d, 512, mod_p, per_token=False)
    grp_s = _Group(bs, ts, d, 512, mod, per_token=True)

    y_p, hg_p, gla_p = _trunk(grp_p, x_prompt.reshape(bp * tp, d), None, None, w,
                              {"hg_bb": 1, "gla_bb": 1, "tb": 512, "hg_gsz": 8, "gla_gsz": 8, "hg_fused": True,
                               "proj_dtype": BF16})
    y_s, hg_s, gla_s = _trunk(grp_s, x_sample.reshape(ms, d), state_hgrn, state_gla, w,
                              {"hg_bb": 32, "gla_bb": 8, "tb": ts, "hg_gsz": 32, "gla_gsz": 8, "hg_fused": False,
                               "proj_dtype": F32})
    return (y_p.reshape(bp, tp, d), y_s.reshape(bs, ts, d), hg_p, gla_p, hg_s, gla_s)
```

```python
import functools

import jax
import jax.numpy as jnp
from jax import lax
from jax.experimental import pallas as pl
from jax.experimental.pallas import tpu as pltpu

F32 = jnp.float32
BF16 = jnp.bfloat16

EPS = 1e-5
LOG2E = 1.4426950408889634
GLA_GATE_NORM = 16.0
LANES = 128
SUBLANES = 8
CHUNK = 16
VMEM_LIMIT = 56 * 1024 * 1024
PIECE_AFTER = (-1, 1)


def _sigmoid(x):
    return 0.5 * jnp.tanh(0.5 * x) + 0.5


def _silu(x):
    return x * _sigmoid(x)


def _log1p_exp_neg(d):
    return jnp.log(1.0 + jnp.exp(-d))


def _log_sigmoid(x):
    return jnp.minimum(x, 0.0) - _log1p_exp_neg(jnp.abs(x))


def _layer_norm(z, w, b):
    mu = jnp.mean(z, axis=-1, keepdims=True)
    zc = z - mu
    var = jnp.mean(zc * zc, axis=-1, keepdims=True)
    return zc * lax.rsqrt(var + EPS) * w + b


def _params(sem, vmem=None):
    return pltpu.CompilerParams(dimension_semantics=sem, vmem_limit_bytes=vmem)


def _ada_kernel(c_ref, w_ref, b_ref, o_ref):
    cs = _silu(c_ref[...]).astype(BF16)
    o_ref[...] = jnp.dot(cs, w_ref[...].astype(BF16), preferred_element_type=F32) + b_ref[...]


def _ada(c_all, w_ada, b_ada):
    depth, d, n6 = w_ada.shape
    rows = c_all.shape[0]
    tn = 1024
    return pl.pallas_call(
        _ada_kernel,
        grid=(depth, n6 // tn),
        in_specs=[pl.BlockSpec((rows, d), lambda l, j: (0, 0)),
                  pl.BlockSpec((None, d, tn), lambda l, j: (l, 0, j)),
                  pl.BlockSpec((None, 1, tn), lambda l, j: (l, 0, j))],
        out_specs=pl.BlockSpec((None, rows, tn), lambda l, j: (l, 0, j)),
        out_shape=jax.ShapeDtypeStruct((depth, rows, n6), F32),
        compiler_params=_params(("arbitrary", "arbitrary"), VMEM_LIMIT),
        name="ada_proj",
    )(c_all, w_ada, b_ada.reshape(depth, 1, n6))


class _Group:
    def __init__(self, batch, seq, d, tm, mod, per_token):
        self.batch, self.seq, self.d, self.tm = batch, seq, d, tm
        self.m = batch * seq
        self.mod = mod
        self.per_token = per_token

    def with_tile(self, tm):
        return _Group(self.batch, self.seq, self.d, tm, self.mod, self.per_token)

    def mod_spec(self, layer, chunk):
        d, tm = self.d, self.tm
        if self.per_token:
            return pl.BlockSpec((None, tm, d), lambda i, *_: (layer, i, chunk))
        tiles_per_b = self.seq // tm
        return pl.BlockSpec((None, None, 1, d), lambda i, *_: (layer, i // tiles_per_b, 0, chunk))


def _mod0_kernel(x_ref, sc_ref, sh_ref, h_ref):
    h_ref[...] = (x_ref[...] * (1.0 + sc_ref[...]) + sh_ref[...]).astype(BF16)


def _mod0(grp, x):
    m, d, tm = grp.m, grp.d, grp.tm
    return pl.pallas_call(
        _mod0_kernel,
        grid=(m // tm,),
        in_specs=[pl.BlockSpec((tm, d), lambda i: (i, 0)), grp.mod_spec(0, 1), grp.mod_spec(0, 0)],
        out_specs=pl.BlockSpec((tm, d), lambda i: (i, 0)),
        out_shape=jax.ShapeDtypeStruct((m, d), BF16),
        compiler_params=_params(("arbitrary",)),
        name="mod0",
    )(x, grp.mod, grp.mod)


def _mm_kernel(h_ref, w_ref, o_ref):
    o_ref[...] = jnp.dot(h_ref[...], w_ref[...], preferred_element_type=F32)


def _mm_cast_kernel(h_ref, w_ref, o_ref, wb_ref):
    @pl.when(pl.program_id(1) == 0)
    def _():
        wb_ref[...] = w_ref[...].astype(BF16)

    o_ref[...] = jnp.dot(h_ref[...], wb_ref[...], preferred_element_type=F32).astype(o_ref.dtype)


def _inproj(grp, h, w, layer, n, out_dtype):
    m, d, tm = grp.m, grp.d, grp.tm
    tn = 1024
    return pl.pallas_call(
        _mm_cast_kernel,
        grid=(n // tn, m // tm),
        in_specs=[pl.BlockSpec((tm, d), lambda j, i: (i, 0)),
                  pl.BlockSpec((None, d, tn), lambda j, i: (layer, 0, j))],
        out_specs=pl.BlockSpec((tm, tn), lambda j, i: (i, j)),
        out_shape=jax.ShapeDtypeStruct((m, n), out_dtype),
        scratch_shapes=[pltpu.VMEM((d, tn), BF16)],
        compiler_params=_params(("arbitrary", "arbitrary"), VMEM_LIMIT),
        name="in_proj",
    )(h, w)


def _regroup_kernel(*refs):
    *in_refs, o_ref = refs
    width = in_refs[0].shape[-1]
    for f, r in enumerate(in_refs):
        o_ref[:, f * width:(f + 1) * width] = r[...].astype(BF16)


def _regroup_heads(w, heads, n_fields):
    layers, d, n = w.shape
    kdim = n // (n_fields * heads)
    return pl.pallas_call(
        _regroup_kernel,
        grid=(layers, heads),
        in_specs=[pl.BlockSpec((None, d, kdim), functools.partial(lambda l, h, f: (l, 0, f * heads + h), f=f))
                  for f in range(n_fields)],
        out_specs=pl.BlockSpec((None, None, d, n_fields * kdim), lambda l, h: (l, h, 0, 0)),
        out_shape=jax.ShapeDtypeStruct((layers, heads, d, n_fields * kdim), BF16),
        compiler_params=_params(("arbitrary", "arbitrary")),
        name="regroup_heads",
    )(*([w] * n_fields))


def _inproj_heads(grp, h, w_heads, layer):
    m, d, tm = grp.m, grp.d, grp.tm
    _, heads, _, width = w_heads.shape
    return pl.pallas_call(
        _mm_kernel,
        grid=(heads, m // tm),
        in_specs=[pl.BlockSpec((tm, d), lambda j, i: (i, 0)),
                  pl.BlockSpec((None, None, d, width), lambda j, i: (layer, j, 0, 0))],
        out_specs=pl.BlockSpec((tm, width), lambda j, i: (i, j)),
        out_shape=jax.ShapeDtypeStruct((m, heads * width), F32),
        compiler_params=_params(("arbitrary", "arbitrary"), VMEM_LIMIT),
        name="in_proj_heads",
    )(h, w_heads)


def _gate_kernel(h_ref, wl_ref, w2_ref, b_ref, o_ref):
    low = jnp.dot(h_ref[...], wl_ref[...], preferred_element_type=F32)
    gk = jnp.dot(low.astype(BF16), w2_ref[...], preferred_element_type=F32) + b_ref[...]
    o_ref[...] = _log_sigmoid(gk) * (1.0 / GLA_GATE_NORM)


def _gla_gate(grp, h, w_low, w_gk2, b_gk, layer):
    m, d, tm = grp.m, grp.d, grp.tm
    kd = w_gk2.shape[-1]
    return pl.pallas_call(
        _gate_kernel,
        grid=(m // tm,),
        in_specs=[pl.BlockSpec((tm, d), lambda i: (i, 0)),
                  pl.BlockSpec((None, d, LANES), lambda i: (layer, 0, 0)),
                  pl.BlockSpec((None, LANES, kd), lambda i: (layer, 0, 0)),
                  pl.BlockSpec((None, 1, kd), lambda i: (layer, 0, 0))],
        out_specs=pl.BlockSpec((tm, kd), lambda i: (i, 0)),
        out_shape=jax.ShapeDtypeStruct((m, kd), F32),
        compiler_params=_params(("arbitrary",)),
        name="gla_gate",
    )(h, w_low, w_gk2, b_gk.reshape(-1, 1, kd))


def _chunk_cumsum(lf):
    g, c, kdim = lf.shape
    y = lf.reshape(g * c // SUBLANES, SUBLANES, kdim)
    row = lax.broadcasted_iota(jnp.int32, y.shape, 1)
    shift = 1
    while shift < SUBLANES:
        y = y + jnp.where(row >= shift, pltpu.roll(y, shift, 1), 0.0)
        shift *= 2
    y = y.reshape(g, c // SUBLANES, SUBLANES, kdim)
    parts = [y[:, 0]]
    for i in range(1, c // SUBLANES):
        parts.append(y[:, i] + parts[-1][:, SUBLANES - 1:SUBLANES, :])
    return jnp.stack(parts, axis=1).reshape(g, c, kdim)


def _chunk_parts(q, k, v, lf, valid):
    g, c, kdim = q.shape
    assert c == 2 * SUBLANES
    b = _chunk_cumsum(lf)
    g2 = 2 * g
    q8, k8, b8 = (x.reshape(g2, SUBLANES, kdim) for x in (q, k, b * LOG2E))
    col = lax.broadcasted_iota(jnp.int32, (g2, SUBLANES, c), 2)
    col0 = (lax.broadcasted_iota(jnp.int32, (g2, SUBLANES, c), 0) % 2) * SUBLANES
    att = jnp.zeros((g2, SUBLANES, c), F32)
    for j in range(min(valid, SUBLANES)):
        w = q8 * k8[:, j:j + 1, :] * jnp.exp2(jnp.minimum(b8 - b8[:, j:j + 1, :], 0.0))
        att = jnp.where(col == col0 + j, jnp.sum(w, axis=2, keepdims=True), att)
    att = jnp.where(col - col0 <= lax.broadcasted_iota(jnp.int32, (g2, SUBLANES, c), 1), att, 0.0)
    att = att.reshape(g, c, c)
    if valid > SUBLANES:
        r = b[:, SUBLANES - 1:SUBLANES, :]
        q_hi = (q * jnp.exp(jnp.minimum(b - r, 0.0))).astype(BF16)
        k_lo = (k * jnp.exp(jnp.minimum(r - b, 0.0))).astype(BF16)
        cross = jnp.einsum("gtk,gsk->gts", q_hi, k_lo, preferred_element_type=F32)
        row16 = lax.broadcasted_iota(jnp.int32, (g, c, c), 1)
        col16 = lax.broadcasted_iota(jnp.int32, (g, c, c), 2)
        att = jnp.where((row16 >= SUBLANES) & (col16 < SUBLANES), cross, att)
    o = jnp.einsum("gts,gsv->gtv", att.astype(BF16), v.astype(BF16), preferred_element_type=F32)
    b_last = b[:, c - 1:c, :]
    return o, q * jnp.exp(b), k * jnp.exp(b_last - b), b_last


def _superblock_chain(q, k, v, lf, s0):
    g, c, _ = q.shape
    vdim = v.shape[-1]
    o, qd, kd, b_last = _chunk_parts(q, k, v, lf, c)
    qd = qd.astype(BF16)
    upd = jnp.einsum("gtk,gtv->gkv", kd.astype(BF16), v.astype(BF16), preferred_element_type=F32)
    e_last = jnp.exp(b_last)
    s = s0
    outs = []
    for i in range(g):
        outs.append(o[i] + jnp.dot(qd[i], s.astype(BF16), preferred_element_type=F32))
        s = s * _decay_columns(e_last[i], vdim) + upd[i]
    return jnp.stack(outs).reshape(g * c, vdim), s


def _level_masks(g, c):
    rows = g * c
    row_i = lax.broadcasted_iota(jnp.int32, (rows, rows), 0)
    col_i = lax.broadcasted_iota(jnp.int32, (rows, rows), 1)
    masks = []
    bit = c
    while bit < rows:
        masks.append(((row_i ^ col_i) < 2 * bit) & ((row_i & bit) != 0) & ((col_i & bit) == 0))
        bit *= 2
    return masks


def _superblock(q, k, v, lf, s0, masks):
    g, c, kdim = q.shape
    vdim = v.shape[-1]
    rows = g * c
    o, qd, kd, b_last = _chunk_parts(q, k, v, lf, c)
    offs = [jnp.zeros((1, kdim), F32)]
    for i in range(1, g):
        offs.append(offs[-1] + b_last[i - 1])
    off = jnp.stack(offs)
    off_end = off + b_last
    total = off_end[g - 1]
    vb = v.astype(BF16).reshape(rows, vdim)
    att = jnp.zeros((rows, rows), F32)
    half = 1
    for take in masks:
        mid = jnp.stack([offs[(i // (2 * half)) * 2 * half + half] for i in range(g)])
        lhs = (qd * jnp.exp(jnp.minimum(off - mid, 0.0))).astype(BF16).reshape(rows, kdim)
        rhs = (kd * jnp.exp(jnp.minimum(mid - off_end, 0.0))).astype(BF16).reshape(rows, kdim)
        a = lax.dot_general(lhs, rhs, (((1,), (1,)), ((), ())), preferred_element_type=F32)
        att = jnp.where(take, a, att)
        half *= 2
    out = o.reshape(rows, vdim)
    if g > 1:
        out = out + jnp.dot(att.astype(BF16), vb, preferred_element_type=F32)
    out = out + jnp.dot((qd * jnp.exp(off)).astype(BF16).reshape(rows, kdim), s0.astype(BF16),
                        preferred_element_type=F32)
    upd = lax.dot_general((kd * jnp.exp(total - off_end)).astype(BF16).reshape(rows, kdim), vb,
                          (((0,), (0,)), ((), ())), preferred_element_type=F32)
    return out, s0 * _decay_columns(jnp.exp(total), vdim) + upd


def _decay_columns(e_row, vdim):
    kdim = e_row.shape[-1]
    e_col = jnp.transpose(jnp.broadcast_to(e_row, (LANES, kdim)))
    if vdim > LANES:
        e_col = jnp.concatenate([e_col] * (vdim // LANES), axis=1)
    return e_col


def _gated_out(o, g, nw):
    ms = jnp.mean(o * o, axis=-1, keepdims=True)
    return (o * lax.rsqrt(ms + EPS) * nw * _silu(g)).astype(BF16)


def _rec_kernel(*refs, mixer, fused, layer_j, has_s0, per_seq_state, gsz, valid, n_sb, n_tblocks):
    refs = list(refs)
    if fused:
        h_ref, w_ref, lb_ref, nw_ref = refs[:4]
        refs = refs[4:]
    elif mixer == "hgrn":
        a_ref, f_ref, v_ref, g_ref, lb_ref, nw_ref = refs[:6]
        refs = refs[6:]
    else:
        a_ref, k_ref, v_ref, g_ref, f_ref, nw_ref = refs[:6]
        refs = refs[6:]
    s0_ref = refs.pop(0) if has_s0 else None
    og_ref, so_ref = refs[:2]
    scr = refs[2:]
    kdim, vdim = so_ref.shape[-2:]
    if fused:
        step = pl.program_id(0)
        ti = lax.rem(jnp.maximum(step - 1, 0), n_tblocks)
    else:
        ti = pl.program_id(2)

    if mixer == "hgrn":
        raw = lb_ref[...]
        ex = jnp.exp(raw - jnp.max(raw, axis=0, keepdims=True))
        p = ex / jnp.sum(ex, axis=0, keepdims=True)
        cum = p[0:1, :]
        for r in range(1, layer_j + 1):
            cum = cum + p[r:r + 1, :]
        lb = jnp.clip(cum - p[0:1, :], 0.0, 1.0 - 1e-6)
        log_lb = jnp.log(lb)
        log_1m = jnp.log1p(-lb)
        one_m = 1.0 - lb
    nw = nw_ref[...]

    def fields(load):
        if fused:
            proj = load(None)
            a, fp, v, g = (proj[..., i * kdim:(i + 1) * kdim] for i in range(4))
        else:
            a, v, g = load(a_ref), load(v_ref), load(g_ref)
        if mixer == "hgrn":
            if not fused:
                fp = load(f_ref)
            q = _silu(a)
            y = log_1m + _log_sigmoid(fp)
            lf = jnp.maximum(log_lb, y) + _log1p_exp_neg(jnp.abs(log_lb - y))
            k = one_m * _sigmoid(-fp)
        else:
            q = a * (kdim ** -0.5)
            k = load(k_ref)
            lf = load(f_ref)
        return q, k, v, lf, g

    if per_seq_state:
        pq, pk, pf, pv, po = scr
        for r in (pq, pk, pf, pv):
            r[...] = jnp.zeros(r.shape, F32)
        q, k, v, lf, g = fields(lambda r: r[...])
        for i in range(gsz):
            seq = slice(i * valid, (i + 1) * valid)
            pq[i, 0:valid, :] = q[seq]
            pk[i, 0:valid, :] = k[seq]
            pf[i, 0:valid, :] = lf[seq]
            pv[i, 0:valid, :] = v[seq]
        o, qd, kd, b_last = _chunk_parts(pq[...], pk[...], pv[...], pf[...], valid)
        qd = qd.astype(BF16)
        upd = jnp.einsum("gtk,gtv->gkv", kd.astype(BF16), pv[...].astype(BF16), preferred_element_type=F32)
        e_last = jnp.exp(b_last)
        for i in range(gsz):
            s = s0_ref[i]
            out = o[i] + jnp.dot(qd[i], s.astype(BF16), preferred_element_type=F32)
            po[i * valid:(i + 1) * valid, :] = out[0:valid]
            so_ref[i] = s * _decay_columns(e_last[i], vdim) + upd[i]
        og_ref[...] = _gated_out(po[...], g, nw)
        return

    s_scr = scr[0]

    @pl.when(ti == 0)
    def _():
        if has_s0:
            s_scr[...] = s0_ref[0]
        else:
            s_scr[...] = jnp.zeros(s_scr.shape, F32)

    rows_sb = gsz * CHUNK

    def time_block(src_ref, after=None):
        hierarchical = kdim * vdim > LANES * LANES
        masks = _level_masks(gsz, CHUNK) if hierarchical else None
        if after and -1 in after:
            after[-1]()
        for sb in range(n_sb):
            rows = pl.ds(sb * rows_sb, rows_sb)

            def load(r):
                x = src_ref[rows, :] if r is None else r[0, rows, :].astype(F32)
                return x.reshape(gsz, CHUNK, x.shape[-1])

            q, k, v, lf, g = fields(load)
            if hierarchical:
                out, s_scr[...] = _superblock(q, k, v, lf, s_scr[...], masks)
            else:
                out, s_scr[...] = _superblock_chain(q, k, v, lf, s_scr[...])
            og_ref[0, rows, :] = _gated_out(out, g.reshape(rows_sb, vdim), nw)
            if after and sb in after:
                after[sb]()

    if not fused:
        time_block(None)

        @pl.when(ti == n_tblocks - 1)
        def _():
            so_ref[0] = s_scr[...]
        return

    proj_a, proj_b = scr[1:]

    @pl.when(step == 0)
    def _():
        proj_a[...] = jnp.dot(h_ref[0], w_ref[...], preferred_element_type=F32)

    odd = lax.rem(step, 2) == 1

    def project_cols(dst_ref, piece, n_pieces):
        width = dst_ref.shape[-1] // n_pieces
        cols = slice(piece * width, (piece + 1) * width)
        dst_ref[:, cols] = jnp.dot(h_ref[0], w_ref[:, cols], preferred_element_type=F32)

    @pl.when(odd)
    def _():
        time_block(proj_a, {sb: functools.partial(project_cols, proj_b, i, 2) for i, sb in enumerate(PIECE_AFTER)})

    @pl.when(jnp.logical_and(jnp.logical_not(odd), step > 0))
    def _():
        time_block(proj_b, {sb: functools.partial(project_cols, proj_a, i, 2) for i, sb in enumerate(PIECE_AFTER)})

    @pl.when(jnp.logical_and(step > 0, ti == n_tblocks - 1))
    def _():
        so_ref[0] = s_scr[...]


def _recurrence(mixer, layer_j, n_layers, proj3, lf3, lb_raw, norm_w, state, state_out, heads, kdim, vdim,
                bb, tb, gsz, w_heads=None):
    bsz, t, _ = proj3.shape
    has_s0 = state is not None
    per_seq_state = t < CHUNK
    fused = w_heads is not None
    n_tblocks = t // tb
    kb, vb = kdim, vdim
    nk = heads * kdim // kb
    nv = heads * vdim // vb

    if per_seq_state:
        proj3 = proj3.reshape(bsz * t, -1)
        lf3 = None if lf3 is None else lf3.reshape(bsz * t, -1)

    if fused:
        n_items = bsz * heads * n_tblocks
        grid = (n_items + 1,)

        def item(lin):
            return lin // (heads * n_tblocks), (lin // n_tblocks) % heads, lin % n_tblocks

        def cur(s):
            return item(jnp.maximum(s - 1, 0))

        def nxt(s):
            return item(jnp.minimum(s, n_items - 1))
    else:
        grid = (bsz // bb, heads, n_tblocks)

        def cur(b, h, ti):
            return b, h, ti

    def tok(width, base, stride=1):
        if per_seq_state:
            return pl.BlockSpec((bb * t, width), lambda *g: (cur(*g)[0], base + stride * cur(*g)[1]))
        return pl.BlockSpec((bb, tb, width), lambda *g: (cur(*g)[0], cur(*g)[2], base + stride * cur(*g)[1]))

    nw_spec = pl.BlockSpec((None, 1, vb), lambda *g: (layer_j, 0, cur(*g)[1]))
    nw3 = norm_w.reshape(norm_w.shape[0], 1, -1)
    lb_spec = None if lb_raw is None else pl.BlockSpec((lb_raw.shape[0], kb), lambda *g: (0, cur(*g)[1]))
    if fused:
        assert mixer == "hgrn" and kdim == vdim and not per_seq_state and not has_s0 and bb == 1
        d = proj3.shape[-1]
        in_specs = [pl.BlockSpec((bb, tb, d), lambda s: (nxt(s)[0], nxt(s)[2], 0)),
                    pl.BlockSpec((None, None, d, 4 * kdim), lambda s: (layer_j, nxt(s)[1], 0, 0)), lb_spec, nw_spec]
        args = [proj3, w_heads, lb_raw, nw3]
    elif mixer == "hgrn":
        assert kdim == vdim
        in_specs = [tok(kb, 0, 4), tok(kb, 1, 4), tok(vb, 2, 4), tok(vb, 3, 4), lb_spec, nw_spec]
        args = [proj3, proj3, proj3, proj3, lb_raw, nw3]
    else:
        v_base = 2 * heads * kdim // vb
        in_specs = [tok(kb, 0), tok(kb, nk), tok(vb, v_base), tok(vb, v_base + nv), tok(kb, 0), nw_spec]
        args = [proj3, proj3, proj3, proj3, lf3, nw3]
    if has_s0:
        in_specs.append(pl.BlockSpec((None, bb, None, kdim, vdim), lambda *g: (layer_j, cur(*g)[0], cur(*g)[1], 0, 0)))
        args.append(state)
    aliases = {}
    if state_out is not None:
        in_specs.append(pl.BlockSpec(memory_space=pl.ANY))
        args.append(state_out)
        aliases = {len(args) - 1: 1}
    if per_seq_state:
        assert has_s0 and tb == t and gsz == bb
        scratch = ([pltpu.VMEM((bb, CHUNK, kdim), F32)] * 3 + [pltpu.VMEM((bb, CHUNK, vdim), F32)]
                   + [pltpu.VMEM((bb * t, vdim), F32)])
        valid, n_sb = t, 1
    else:
        assert bb == 1 and tb % (gsz * CHUNK) == 0
        scratch = [pltpu.VMEM((kdim, vdim), F32)]
        if fused:
            scratch += [pltpu.VMEM((tb, 4 * kdim), F32)] * 2
        valid, n_sb = CHUNK, tb // (gsz * CHUNK)
    if per_seq_state:
        og_spec = pl.BlockSpec((bb * t, vb), lambda *g: (cur(*g)[0], cur(*g)[1]))
        og_shape = (bsz * t, heads * vdim)
    else:
        og_spec = pl.BlockSpec((bb, tb, vb), lambda *g: (cur(*g)[0], cur(*g)[2], cur(*g)[1]))
        og_shape = (bsz, t, heads * vdim)

    def kern(*refs):
        if state_out is not None:
            refs = refs[:len(args) - 1] + refs[len(args):]
        _rec_kernel(*refs, mixer=mixer, fused=fused, layer_j=layer_j, has_s0=has_s0, per_seq_state=per_seq_state,
                    gsz=gsz, valid=valid, n_sb=n_sb, n_tblocks=n_tblocks)

    return pl.pallas_call(
        kern,
        grid=grid,
        in_specs=in_specs,
        out_specs=[og_spec,
                   pl.BlockSpec((None, bb, None, kdim, vdim), lambda *g: (layer_j, cur(*g)[0], cur(*g)[1], 0, 0))],
        out_shape=[jax.ShapeDtypeStruct(og_shape, BF16),
                   jax.ShapeDtypeStruct((n_layers, bsz, heads, kdim, vdim), F32)],
        scratch_shapes=scratch,
        input_output_aliases=aliases,
        compiler_params=_params(("arbitrary",) * len(grid), VMEM_LIMIT),
        name=mixer + "_rec",
    )(*args)


def _outproj_kernel(og_ref, w_ref, x_ref, g_ref, sc_ref, sh_ref, lnw_ref, lnb_ref, xo_ref, ho_ref, *, alpha):
    y = jnp.dot(og_ref[...], w_ref[...], preferred_element_type=F32)
    xn = _layer_norm(alpha * x_ref[...] + (1.0 + g_ref[...]) * y, lnw_ref[...], lnb_ref[...])
    xo_ref[...] = xn
    ho_ref[...] = (xn * (1.0 + sc_ref[...]) + sh_ref[...]).astype(BF16)


def _ln_specs(layer, sub, d, n_grid):
    idx = (lambda i: (layer * 2 + sub, 0, 0)) if n_grid == 1 else (lambda i, f: (layer * 2 + sub, 0, 0))
    return [pl.BlockSpec((None, 1, d), idx), pl.BlockSpec((None, 1, d), idx)]


def _outproj(grp, og, w_out, layer_j, x, layer, ln_w, ln_b, alpha):
    m, d = grp.m, grp.d
    sub = grp.with_tile(min(grp.tm, 256))
    tm = sub.tm
    din = w_out.shape[1]
    row = lambda i: (i, 0)
    return pl.pallas_call(
        functools.partial(_outproj_kernel, alpha=alpha),
        grid=(m // tm,),
        in_specs=[pl.BlockSpec((tm, din), row), pl.BlockSpec((None, din, d), lambda i: (layer_j, 0, 0)),
                  pl.BlockSpec((tm, d), row), sub.mod_spec(layer, 2), sub.mod_spec(layer, 4),
                  sub.mod_spec(layer, 3)] + _ln_specs(layer, 0, d, 1),
        out_specs=[pl.BlockSpec((tm, d), row), pl.BlockSpec((tm, d), row)],
        out_shape=[jax.ShapeDtypeStruct((m, d), F32), jax.ShapeDtypeStruct((m, d), BF16)],
        compiler_params=_params(("arbitrary",), VMEM_LIMIT),
        name="out_proj_ln",
    )(og, w_out, x, grp.mod, grp.mod, grp.mod, ln_w, ln_b)


def _mlp_kernel(*refs, alpha, n_f, emit_h):
    if emit_h:
        h_ref, wu_ref, wd_ref, x_ref, g_ref, sc_ref, sh_ref, lnw_ref, lnb_ref, xo_ref, ho_ref, acc_ref = refs
    else:
        h_ref, wu_ref, wd_ref, x_ref, g_ref, lnw_ref, lnb_ref, xo_ref, acc_ref = refs
    f = pl.program_id(1)

    @pl.when(f == 0)
    def _():
        acc_ref[...] = jnp.zeros(acc_ref.shape, F32)

    u = jnp.dot(h_ref[...], wu_ref[...], preferred_element_type=F32)
    u = jnp.square(jnp.maximum(u, 0.0)).astype(BF16)
    acc_ref[...] += jnp.dot(u, wd_ref[...], preferred_element_type=F32)

    @pl.when(f == n_f - 1)
    def _():
        xn = _layer_norm(alpha * x_ref[...] + (1.0 + g_ref[...]) * acc_ref[...], lnw_ref[...], lnb_ref[...])
        xo_ref[...] = xn
        if emit_h:
            ho_ref[...] = (xn * (1.0 + sc_ref[...]) + sh_ref[...]).astype(BF16)


def _mlp(grp, h, w_up, w_down, x, layer, ln_w, ln_b, alpha, emit_h):
    m, d, tm = grp.m, grp.d, grp.tm
    dff = w_up.shape[-1]
    tf = 1024
    n_f = dff // tf
    row = lambda i, f: (i, 0)
    in_specs = [pl.BlockSpec((tm, d), row), pl.BlockSpec((None, d, tf), lambda i, f: (layer, 0, f)),
                pl.BlockSpec((None, tf, d), lambda i, f: (layer, f, 0)), pl.BlockSpec((tm, d), row),
                grp.mod_spec(layer, 5)]
    args = [h, w_up, w_down, x, grp.mod]
    out_specs = [pl.BlockSpec((tm, d), row)]
    out_shape = [jax.ShapeDtypeStruct((m, d), F32)]
    if emit_h:
        in_specs += [grp.mod_spec(layer + 1, 1), grp.mod_spec(layer + 1, 0)]
        args += [grp.mod, grp.mod]
        out_specs.append(pl.BlockSpec((tm, d), row))
        out_shape.append(jax.ShapeDtypeStruct((m, d), BF16))
    in_specs += _ln_specs(layer, 1, d, 2)
    args += [ln_w, ln_b]
    outs = pl.pallas_call(
        functools.partial(_mlp_kernel, alpha=alpha, n_f=n_f, emit_h=emit_h),
        grid=(m // tm, n_f),
        in_specs=in_specs,
        out_specs=out_specs,
        out_shape=out_shape,
        scratch_shapes=[pltpu.VMEM((tm, d), F32)],
        compiler_params=_params(("arbitrary", "arbitrary"), VMEM_LIMIT),
        name="mlp_ln",
    )(*args)
    return (outs[0], outs[1]) if emit_h else (outs[0], None)


def _trunk(grp, x, s_hg, s_gla, w, rec_cfg):
    depth = w["w_up"].shape[0]
    alpha = (2.0 * depth) ** 0.25
    hg_heads, hg_k, hg_v = w["hg_dims"]
    gla_heads, gla_k, gla_v = w["gla_dims"]
    n_hg, n_gla = w["hg_w_out"].shape[0], w["gla_w_out"].shape[0]
    new_hg = new_gla = None
    h = _mod0(grp, x)
    for l in range(depth):
        j = l // 2
        if l % 2 == 0:
            if rec_cfg["hg_fused"]:
                src, w_heads = h, w["hg_w_heads"]
            else:
                src, w_heads = _inproj_heads(grp, h, w["hg_w_heads"], j), None
            og, new_hg = _recurrence("hgrn", j, n_hg, src.reshape(grp.batch, grp.seq, -1), None, w["hg_lb_raw"],
                                     w["hg_norm_w"], s_hg, new_hg, hg_heads, hg_k, hg_v,
                                     rec_cfg["hg_bb"], rec_cfg["tb"], rec_cfg["hg_gsz"], w_heads)
            w_out = w["hg_w_out"]
        else:
            proj = _inproj(grp, h, w["gla_w_in"], j, w["gla_main"], rec_cfg["proj_dtype"])
            lf = _gla_gate(grp, h, w["gla_w_low"], w["gla_w_gk2"], w["gla_b_gk"], j)
            og, new_gla = _recurrence("gla", j, n_gla, proj.reshape(grp.batch, grp.seq, -1),
                                      lf.reshape(grp.batch, grp.seq, -1), None, w["gla_norm_w"], s_gla, new_gla,
                                      gla_heads, gla_k, gla_v, rec_cfg["gla_bb"], rec_cfg["tb"], rec_cfg["gla_gsz"])
            w_out = w["gla_w_out"]
        x, h2 = _outproj(grp, og.reshape(grp.m, -1), w_out, j, x, l, w["ln_w"], w["ln_b"], alpha)
        x, h = _mlp(grp, h2, w["w_up"], w["w_down"], x, l, w["ln_w"], w["ln_b"], alpha, emit_h=(l + 1 < depth))
    return x, new_hg, new_gla


def kernel(x_prompt, x_sample, state_hgrn, state_gla, c_prompt, c_sample, w_ada, b_ada, ln_w, ln_b,
           hg_w_in, hg_lb_raw, hg_norm_w, hg_w_out, gla_w_in, gla_w_gk2, gla_b_gk, gla_norm_w,
           gla_w_out, w_up, w_down):
    bp, tp, d = x_prompt.shape
    bs, ts, _ = x_sample.shape
    depth = w_up.shape[0]
    _, _, hg_heads, hg_k, hg_v = state_hgrn.shape
    _, _, gla_heads, gla_k, gla_v = state_gla.shape
    gla_main = 2 * gla_heads * gla_k + 2 * gla_heads * gla_v
    rank = gla_w_in.shape[-1] - gla_main

    ms = bs * ts
    pad_rows = (-(ms + bp)) % 8
    c_all = jnp.concatenate([jnp.repeat(c_sample, ts, axis=0), c_prompt, jnp.zeros((pad_rows, d), F32)], axis=0)
    mod = _ada(c_all, w_ada, b_ada)
    mod_p = mod[:, ms:ms + bp].reshape(depth, bp, 1, 6 * d)

    w = {
        "hg_w_heads": _regroup_heads(hg_w_in, hg_heads, 4), "hg_lb_raw": hg_lb_raw, "hg_norm_w": hg_norm_w,
        "hg_w_out": hg_w_out.astype(BF16),
        "gla_w_in": gla_w_in, "gla_main": gla_main,
        "gla_w_low": jnp.pad(gla_w_in[..., gla_main:], ((0, 0), (0, 0), (0, LANES - rank))).astype(BF16),
        "gla_w_gk2": jnp.pad(gla_w_gk2, ((0, 0), (0, LANES - rank), (0, 0))).astype(BF16),
        "gla_b_gk": gla_b_gk, "gla_norm_w": gla_norm_w, "gla_w_out": gla_w_out.astype(BF16),
        "w_up": w_up.astype(BF16), "w_down": w_down.astype(BF16),
        "ln_w": ln_w.reshape(depth * 2, 1, d), "ln_b": ln_b.reshape(depth * 2, 1, d),
        "hg_dims": (hg_heads, hg_k, hg_v), "gla_dims": (gla_heads, gla_k, gla_v),
    }

    grp_p = _Group(bp, tp, d, 512, mod_p, per_token=False)
    grp_s = _Group(bs, ts, d, 512, mod, per_token=True)

    y_p, hg_p, gla_p = _trunk(grp_p, x_prompt.reshape(bp * tp, d), None, None, w,
                              {"hg_bb": 1, "gla_bb": 1, "tb": 512, "hg_gsz": 8, "gla_gsz": 8, "hg_fused": True,
                               "proj_dtype": BF16})
    y_s, hg_s, gla_s = _trunk(grp_s, x_sample.reshape(ms, d), state_hgrn, state_gla, w,
                              {"hg_bb": 32, "gla_bb": 8, "tb": ts, "hg_gsz": 32, "gla_gsz": 8, "hg_fused": False,
                               "proj_dtype": F32})
    return (y_p.reshape(bp, tp, d), y_s.reshape(bs, ts, d), hg_p, gla_p, hg_s, gla_s)
```

```python
import functools

import jax
import jax.numpy as jnp
from jax import lax
from jax.experimental import pallas as pl
from jax.experimental.pallas import tpu as pltpu

F32 = jnp.float32
BF16 = jnp.bfloat16

EPS = 1e-5
LOG2E = 1.4426950408889634
GLA_GATE_NORM = 16.0
LANES = 128
SUBLANES = 8
CHUNK = 16
VMEM_LIMIT = 56 * 1024 * 1024


def _sigmoid(x):
    return 0.5 * jnp.tanh(0.5 * x) + 0.5


def _silu(x):
    return x * _sigmoid(x)


def _log1p_exp_neg(d):
    return jnp.log(1.0 + jnp.exp(-d))


def _log_sigmoid(x):
    return jnp.minimum(x, 0.0) - _log1p_exp_neg(jnp.abs(x))


def _layer_norm(z, w, b):
    mu = jnp.mean(z, axis=-1, keepdims=True)
    zc = z - mu
    var = jnp.mean(zc * zc, axis=-1, keepdims=True)
    return zc * lax.rsqrt(var + EPS) * w + b


def _params(sem, vmem=None):
    return pltpu.CompilerParams(dimension_semantics=sem, vmem_limit_bytes=vmem)


def _ada_kernel(c_ref, w_ref, b_ref, o_ref):
    cs = _silu(c_ref[...]).astype(BF16)
    o_ref[...] = jnp.dot(cs, w_ref[...].astype(BF16), preferred_element_type=F32) + b_ref[...]


def _ada(c_all, w_ada, b_ada):
    depth, d, n6 = w_ada.shape
    rows = c_all.shape[0]
    tn = 1024
    return pl.pallas_call(
        _ada_kernel,
        grid=(depth, n6 // tn),
        in_specs=[pl.BlockSpec((rows, d), lambda l, j: (0, 0)),
                  pl.BlockSpec((None, d, tn), lambda l, j: (l, 0, j)),
                  pl.BlockSpec((None, 1, tn), lambda l, j: (l, 0, j))],
        out_specs=pl.BlockSpec((None, rows, tn), lambda l, j: (l, 0, j)),
        out_shape=jax.ShapeDtypeStruct((depth, rows, n6), F32),
        compiler_params=_params(("arbitrary", "arbitrary"), VMEM_LIMIT),
        name="ada_proj",
    )(c_all, w_ada, b_ada.reshape(depth, 1, n6))


class _Group:
    def __init__(self, batch, seq, d, tm, mod, per_token):
        self.batch, self.seq, self.d, self.tm = batch, seq, d, tm
        self.m = batch * seq
        self.mod = mod
        self.per_token = per_token

    def with_tile(self, tm):
        return _Group(self.batch, self.seq, self.d, tm, self.mod, self.per_token)

    def mod_spec(self, layer, chunk):
        d, tm = self.d, self.tm
        if self.per_token:
            return pl.BlockSpec((None, tm, d), lambda i, *_: (layer, i, chunk))
        tiles_per_b = self.seq // tm
        return pl.BlockSpec((None, None, 1, d), lambda i, *_: (layer, i // tiles_per_b, 0, chunk))


def _mod0_kernel(x_ref, sc_ref, sh_ref, h_ref):
    h_ref[...] = (x_ref[...] * (1.0 + sc_ref[...]) + sh_ref[...]).astype(BF16)


def _mod0(grp, x):
    m, d, tm = grp.m, grp.d, grp.tm
    return pl.pallas_call(
        _mod0_kernel,
        grid=(m // tm,),
        in_specs=[pl.BlockSpec((tm, d), lambda i: (i, 0)), grp.mod_spec(0, 1), grp.mod_spec(0, 0)],
        out_specs=pl.BlockSpec((tm, d), lambda i: (i, 0)),
        out_shape=jax.ShapeDtypeStruct((m, d), BF16),
        compiler_params=_params(("arbitrary",)),
        name="mod0",
    )(x, grp.mod, grp.mod)


def _mm_kernel(h_ref, w_ref, o_ref):
    o_ref[...] = jnp.dot(h_ref[...], w_ref[...], preferred_element_type=F32)


def _mm_cast_kernel(h_ref, w_ref, o_ref, wb_ref):
    @pl.when(pl.program_id(1) == 0)
    def _():
        wb_ref[...] = w_ref[...].astype(BF16)

    o_ref[...] = jnp.dot(h_ref[...], wb_ref[...], preferred_element_type=F32).astype(o_ref.dtype)


def _inproj(grp, h, w, layer, n, out_dtype):
    m, d, tm = grp.m, grp.d, grp.tm
    tn = 1024
    return pl.pallas_call(
        _mm_cast_kernel,
        grid=(n // tn, m // tm),
        in_specs=[pl.BlockSpec((tm, d), lambda j, i: (i, 0)),
                  pl.BlockSpec((None, d, tn), lambda j, i: (layer, 0, j))],
        out_specs=pl.BlockSpec((tm, tn), lambda j, i: (i, j)),
        out_shape=jax.ShapeDtypeStruct((m, n), out_dtype),
        scratch_shapes=[pltpu.VMEM((d, tn), BF16)],
        compiler_params=_params(("arbitrary", "arbitrary"), VMEM_LIMIT),
        name="in_proj",
    )(h, w)


def _regroup_kernel(*refs):
    *in_refs, o_ref = refs
    width = in_refs[0].shape[-1]
    for f, r in enumerate(in_refs):
        o_ref[:, f * width:(f + 1) * width] = r[...].astype(BF16)


def _regroup_heads(w, heads, n_fields):
    layers, d, n = w.shape
    kdim = n // (n_fields * heads)
    return pl.pallas_call(
        _regroup_kernel,
        grid=(layers, heads),
        in_specs=[pl.BlockSpec((None, d, kdim), functools.partial(lambda l, h, f: (l, 0, f * heads + h), f=f))
                  for f in range(n_fields)],
        out_specs=pl.BlockSpec((None, None, d, n_fields * kdim), lambda l, h: (l, h, 0, 0)),
        out_shape=jax.ShapeDtypeStruct((layers, heads, d, n_fields * kdim), BF16),
        compiler_params=_params(("arbitrary", "arbitrary")),
        name="regroup_heads",
    )(*([w] * n_fields))


def _inproj_heads(grp, h, w_heads, layer):
    m, d, tm = grp.m, grp.d, grp.tm
    _, heads, _, width = w_heads.shape
    return pl.pallas_call(
        _mm_kernel,
        grid=(heads, m // tm),
        in_specs=[pl.BlockSpec((tm, d), lambda j, i: (i, 0)),
                  pl.BlockSpec((None, None, d, width), lambda j, i: (layer, j, 0, 0))],
        out_specs=pl.BlockSpec((tm, width), lambda j, i: (i, j)),
        out_shape=jax.ShapeDtypeStruct((m, heads * width), F32),
        compiler_params=_params(("arbitrary", "arbitrary"), VMEM_LIMIT),
        name="in_proj_heads",
    )(h, w_heads)


def _gate_kernel(h_ref, wl_ref, w2_ref, b_ref, o_ref):
    low = jnp.dot(h_ref[...], wl_ref[...], preferred_element_type=F32)
    gk = jnp.dot(low.astype(BF16), w2_ref[...], preferred_element_type=F32) + b_ref[...]
    o_ref[...] = _log_sigmoid(gk) * (1.0 / GLA_GATE_NORM)


def _gla_gate(grp, h, w_low, w_gk2, b_gk, layer):
    m, d, tm = grp.m, grp.d, grp.tm
    kd = w_gk2.shape[-1]
    return pl.pallas_call(
        _gate_kernel,
        grid=(m // tm,),
        in_specs=[pl.BlockSpec((tm, d), lambda i: (i, 0)),
                  pl.BlockSpec((None, d, LANES), lambda i: (layer, 0, 0)),
                  pl.BlockSpec((None, LANES, kd), lambda i: (layer, 0, 0)),
                  pl.BlockSpec((None, 1, kd), lambda i: (layer, 0, 0))],
        out_specs=pl.BlockSpec((tm, kd), lambda i: (i, 0)),
        out_shape=jax.ShapeDtypeStruct((m, kd), F32),
        compiler_params=_params(("arbitrary",)),
        name="gla_gate",
    )(h, w_low, w_gk2, b_gk.reshape(-1, 1, kd))


def _chunk_cumsum(lf):
    g, c, kdim = lf.shape
    y = lf.reshape(g * c // SUBLANES, SUBLANES, kdim)
    row = lax.broadcasted_iota(jnp.int32, y.shape, 1)
    shift = 1
    while shift < SUBLANES:
        y = y + jnp.where(row >= shift, pltpu.roll(y, shift, 1), 0.0)
        shift *= 2
    y = y.reshape(g, c // SUBLANES, SUBLANES, kdim)
    parts = [y[:, 0]]
    for i in range(1, c // SUBLANES):
        parts.append(y[:, i] + parts[-1][:, SUBLANES - 1:SUBLANES, :])
    return jnp.stack(parts, axis=1).reshape(g, c, kdim)


def _chunk_parts(q, k, v, lf, valid):
    g, c, kdim = q.shape
    assert c == 2 * SUBLANES
    b = _chunk_cumsum(lf)
    g2 = 2 * g
    q8, k8, b8 = (x.reshape(g2, SUBLANES, kdim) for x in (q, k, b * LOG2E))
    col = lax.broadcasted_iota(jnp.int32, (g2, SUBLANES, c), 2)
    col0 = (lax.broadcasted_iota(jnp.int32, (g2, SUBLANES, c), 0) % 2) * SUBLANES
    att = jnp.zeros((g2, SUBLANES, c), F32)
    for j in range(min(valid, SUBLANES)):
        w = q8 * k8[:, j:j + 1, :] * jnp.exp2(jnp.minimum(b8 - b8[:, j:j + 1, :], 0.0))
        att = jnp.where(col == col0 + j, jnp.sum(w, axis=2, keepdims=True), att)
    att = jnp.where(col - col0 <= lax.broadcasted_iota(jnp.int32, (g2, SUBLANES, c), 1), att, 0.0)
    att = att.reshape(g, c, c)
    if valid > SUBLANES:
        r = b[:, SUBLANES - 1:SUBLANES, :]
        q_hi = (q * jnp.exp(jnp.minimum(b - r, 0.0))).astype(BF16)
        k_lo = (k * jnp.exp(jnp.minimum(r - b, 0.0))).astype(BF16)
        cross = jnp.einsum("gtk,gsk->gts", q_hi, k_lo, preferred_element_type=F32)
        row16 = lax.broadcasted_iota(jnp.int32, (g, c, c), 1)
        col16 = lax.broadcasted_iota(jnp.int32, (g, c, c), 2)
        att = jnp.where((row16 >= SUBLANES) & (col16 < SUBLANES), cross, att)
    o = jnp.einsum("gts,gsv->gtv", att.astype(BF16), v.astype(BF16), preferred_element_type=F32)
    b_last = b[:, c - 1:c, :]
    return o, q * jnp.exp(b), k * jnp.exp(b_last - b), b_last


def _superblock_chain(q, k, v, lf, s0):
    g, c, _ = q.shape
    vdim = v.shape[-1]
    o, qd, kd, b_last = _chunk_parts(q, k, v, lf, c)
    qd = qd.astype(BF16)
    upd = jnp.einsum("gtk,gtv->gkv", kd.astype(BF16), v.astype(BF16), preferred_element_type=F32)
    e_last = jnp.exp(b_last)
    s = s0
    outs = []
    for i in range(g):
        outs.append(o[i] + jnp.dot(qd[i], s.astype(BF16), preferred_element_type=F32))
        s = s * _decay_columns(e_last[i], vdim) + upd[i]
    return jnp.stack(outs).reshape(g * c, vdim), s


def _level_masks(g, c):
    rows = g * c
    row_i = lax.broadcasted_iota(jnp.int32, (rows, rows), 0)
    col_i = lax.broadcasted_iota(jnp.int32, (rows, rows), 1)
    masks = []
    bit = c
    while bit < rows:
        masks.append(((row_i ^ col_i) < 2 * bit) & ((row_i & bit) != 0) & ((col_i & bit) == 0))
        bit *= 2
    return masks


def _superblock(q, k, v, lf, s0, masks):
    g, c, kdim = q.shape
    vdim = v.shape[-1]
    rows = g * c
    o, qd, kd, b_last = _chunk_parts(q, k, v, lf, c)
    offs = [jnp.zeros((1, kdim), F32)]
    for i in range(1, g):
        offs.append(offs[-1] + b_last[i - 1])
    off = jnp.stack(offs)
    off_end = off + b_last
    total = off_end[g - 1]
    vb = v.astype(BF16).reshape(rows, vdim)
    att = jnp.zeros((rows, rows), F32)
    half = 1
    for take in masks:
        mid = jnp.stack([offs[(i // (2 * half)) * 2 * half + half] for i in range(g)])
        lhs = (qd * jnp.exp(jnp.minimum(off - mid, 0.0))).astype(BF16).reshape(rows, kdim)
        rhs = (kd * jnp.exp(jnp.minimum(mid - off_end, 0.0))).astype(BF16).reshape(rows, kdim)
        a = lax.dot_general(lhs, rhs, (((1,), (1,)), ((), ())), preferred_element_type=F32)
        att = jnp.where(take, a, att)
        half *= 2
    out = o.reshape(rows, vdim)
    if g > 1:
        out = out + jnp.dot(att.astype(BF16), vb, preferred_element_type=F32)
    out = out + jnp.dot((qd * jnp.exp(off)).astype(BF16).reshape(rows, kdim), s0.astype(BF16),
                        preferred_element_type=F32)
    upd = lax.dot_general((kd * jnp.exp(total - off_end)).astype(BF16).reshape(rows, kdim), vb,
                          (((0,), (0,)), ((), ())), preferred_element_type=F32)
    return out, s0 * _decay_columns(jnp.exp(total), vdim) + upd


def _decay_columns(e_row, vdim):
    kdim = e_row.shape[-1]
    e_col = jnp.transpose(jnp.broadcast_to(e_row, (LANES, kdim)))
    if vdim > LANES:
        e_col = jnp.concatenate([e_col] * (vdim // LANES), axis=1)
    return e_col


def _gated_out(o, g, nw):
    ms = jnp.mean(o * o, axis=-1, keepdims=True)
    return (o * lax.rsqrt(ms + EPS) * nw * _silu(g)).astype(BF16)


def _rec_kernel(*refs, mixer, fused, layer_j, has_s0, per_seq_state, gsz, valid, n_sb, n_tblocks):
    refs = list(refs)
    if fused:
        h_ref, w_ref, lb_ref, nw_ref = refs[:4]
        refs = refs[4:]
    elif mixer == "hgrn":
        a_ref, f_ref, v_ref, g_ref, lb_ref, nw_ref = refs[:6]
        refs = refs[6:]
    else:
        a_ref, k_ref, v_ref, g_ref, f_ref, nw_ref = refs[:6]
        refs = refs[6:]
    s0_ref = refs.pop(0) if has_s0 else None
    og_ref, so_ref = refs[:2]
    scr = refs[2:]
    kdim, vdim = so_ref.shape[-2:]
    if fused:
        step = pl.program_id(0)
        ti = lax.rem(jnp.maximum(step - 1, 0), n_tblocks)
    else:
        ti = pl.program_id(2)

    if mixer == "hgrn":
        raw = lb_ref[...]
        ex = jnp.exp(raw - jnp.max(raw, axis=0, keepdims=True))
        p = ex / jnp.sum(ex, axis=0, keepdims=True)
        cum = p[0:1, :]
        for r in range(1, layer_j + 1):
            cum = cum + p[r:r + 1, :]
        lb = jnp.clip(cum - p[0:1, :], 0.0, 1.0 - 1e-6)
        log_lb = jnp.log(lb)
        log_1m = jnp.log1p(-lb)
        one_m = 1.0 - lb
    nw = nw_ref[...]

    def fields(load):
        if fused:
            proj = load(None)
            a, fp, v, g = (proj[..., i * kdim:(i + 1) * kdim] for i in range(4))
        else:
            a, v, g = load(a_ref), load(v_ref), load(g_ref)
        if mixer == "hgrn":
            if not fused:
                fp = load(f_ref)
            q = _silu(a)
            y = log_1m + _log_sigmoid(fp)
            lf = jnp.maximum(log_lb, y) + _log1p_exp_neg(jnp.abs(log_lb - y))
            k = one_m * _sigmoid(-fp)
        else:
            q = a * (kdim ** -0.5)
            k = load(k_ref)
            lf = load(f_ref)
        return q, k, v, lf, g

    if per_seq_state:
        pq, pk, pf, pv, po = scr
        for r in (pq, pk, pf, pv):
            r[...] = jnp.zeros(r.shape, F32)
        q, k, v, lf, g = fields(lambda r: r[...])
        for i in range(gsz):
            seq = slice(i * valid, (i + 1) * valid)
            pq[i, 0:valid, :] = q[seq]
            pk[i, 0:valid, :] = k[seq]
            pf[i, 0:valid, :] = lf[seq]
            pv[i, 0:valid, :] = v[seq]
        o, qd, kd, b_last = _chunk_parts(pq[...], pk[...], pv[...], pf[...], valid)
        qd = qd.astype(BF16)
        upd = jnp.einsum("gtk,gtv->gkv", kd.astype(BF16), pv[...].astype(BF16), preferred_element_type=F32)
        e_last = jnp.exp(b_last)
        for i in range(gsz):
            s = s0_ref[i]
            out = o[i] + jnp.dot(qd[i], s.astype(BF16), preferred_element_type=F32)
            po[i * valid:(i + 1) * valid, :] = out[0:valid]
            so_ref[i] = s * _decay_columns(e_last[i], vdim) + upd[i]
        og_ref[...] = _gated_out(po[...], g, nw)
        return

    s_scr = scr[0]

    @pl.when(ti == 0)
    def _():
        if has_s0:
            s_scr[...] = s0_ref[0]
        else:
            s_scr[...] = jnp.zeros(s_scr.shape, F32)

    rows_sb = gsz * CHUNK

    def time_block(src_ref, after=None):
        hierarchical = kdim * vdim > LANES * LANES
        masks = _level_masks(gsz, CHUNK) if hierarchical else None
        if after and -1 in after:
            after[-1]()
        for sb in range(n_sb):
            rows = pl.ds(sb * rows_sb, rows_sb)

            def load(r):
                x = src_ref[rows, :] if r is None else r[0, rows, :].astype(F32)
                return x.reshape(gsz, CHUNK, x.shape[-1])

            q, k, v, lf, g = fields(load)
            if hierarchical:
                out, s_scr[...] = _superblock(q, k, v, lf, s_scr[...], masks)
            else:
                out, s_scr[...] = _superblock_chain(q, k, v, lf, s_scr[...])
            og_ref[0, rows, :] = _gated_out(out, g.reshape(rows_sb, vdim), nw)
            if after and sb in after:
                after[sb]()

    if not fused:
        time_block(None)

        @pl.when(ti == n_tblocks - 1)
        def _():
            so_ref[0] = s_scr[...]
        return

    proj_a, proj_b = scr[1:]

    @pl.when(step == 0)
    def _():
        proj_a[...] = jnp.dot(h_ref[0], w_ref[...], preferred_element_type=F32)

    odd = lax.rem(step, 2) == 1

    def project_cols(dst_ref, piece, n_pieces):
        width = dst_ref.shape[-1] // n_pieces
        cols = slice(piece * width, (piece + 1) * width)
        dst_ref[:, cols] = jnp.dot(h_ref[0], w_ref[:, cols], preferred_element_type=F32)

    piece_after = (-1, n_sb // 2 - 1)

    @pl.when(odd)
    def _():
        time_block(proj_a, {sb: functools.partial(project_cols, proj_b, i, 2) for i, sb in enumerate(piece_after)})

    @pl.when(jnp.logical_and(jnp.logical_not(odd), step > 0))
    def _():
        time_block(proj_b, {sb: functools.partial(project_cols, proj_a, i, 2) for i, sb in enumerate(piece_after)})

    @pl.when(jnp.logical_and(step > 0, ti == n_tblocks - 1))
    def _():
        so_ref[0] = s_scr[...]


def _recurrence(mixer, layer_j, n_layers, proj3, lf3, lb_raw, norm_w, state, state_out, heads, kdim, vdim,
                bb, tb, gsz, w_heads=None):
    bsz, t, _ = proj3.shape
    has_s0 = state is not None
    per_seq_state = t < CHUNK
    fused = w_heads is not None
    n_tblocks = t // tb
    kb, vb = kdim, vdim
    nk = heads * kdim // kb
    nv = heads * vdim // vb

    if per_seq_state:
        proj3 = proj3.reshape(bsz * t, -1)
        lf3 = None if lf3 is None else lf3.reshape(bsz * t, -1)

    if fused:
        n_items = bsz * heads * n_tblocks
        grid = (n_items + 1,)

        def item(lin):
            return lin // (heads * n_tblocks), (lin // n_tblocks) % heads, lin % n_tblocks

        def cur(s):
            return item(jnp.maximum(s - 1, 0))

        def nxt(s):
            return item(jnp.minimum(s, n_items - 1))
    else:
        grid = (bsz // bb, heads, n_tblocks)

        def cur(b, h, ti):
            return b, h, ti

    def tok(width, base, stride=1):
        if per_seq_state:
            return pl.BlockSpec((bb * t, width), lambda *g: (cur(*g)[0], base + stride * cur(*g)[1]))
        return pl.BlockSpec((bb, tb, width), lambda *g: (cur(*g)[0], cur(*g)[2], base + stride * cur(*g)[1]))

    nw_spec = pl.BlockSpec((None, 1, vb), lambda *g: (layer_j, 0, cur(*g)[1]))
    nw3 = norm_w.reshape(norm_w.shape[0], 1, -1)
    lb_spec = None if lb_raw is None else pl.BlockSpec((lb_raw.shape[0], kb), lambda *g: (0, cur(*g)[1]))
    if fused:
        assert mixer == "hgrn" and kdim == vdim and not per_seq_state and not has_s0 and bb == 1
        d = proj3.shape[-1]
        in_specs = [pl.BlockSpec((bb, tb, d), lambda s: (nxt(s)[0], nxt(s)[2], 0)),
                    pl.BlockSpec((None, None, d, 4 * kdim), lambda s: (layer_j, nxt(s)[1], 0, 0)), lb_spec, nw_spec]
        args = [proj3, w_heads, lb_raw, nw3]
    elif mixer == "hgrn":
        assert kdim == vdim
        in_specs = [tok(kb, 0, 4), tok(kb, 1, 4), tok(vb, 2, 4), tok(vb, 3, 4), lb_spec, nw_spec]
        args = [proj3, proj3, proj3, proj3, lb_raw, nw3]
    else:
        v_base = 2 * heads * kdim // vb
        in_specs = [tok(kb, 0), tok(kb, nk), tok(vb, v_base), tok(vb, v_base + nv), tok(kb, 0), nw_spec]
        args = [proj3, proj3, proj3, proj3, lf3, nw3]
    if has_s0:
        in_specs.append(pl.BlockSpec((None, bb, None, kdim, vdim), lambda *g: (layer_j, cur(*g)[0], cur(*g)[1], 0, 0)))
        args.append(state)
    aliases = {}
    if state_out is not None:
        in_specs.append(pl.BlockSpec(memory_space=pl.ANY))
        args.append(state_out)
        aliases = {len(args) - 1: 1}
    if per_seq_state:
        assert has_s0 and tb == t and gsz == bb
        scratch = ([pltpu.VMEM((bb, CHUNK, kdim), F32)] * 3 + [pltpu.VMEM((bb, CHUNK, vdim), F32)]
                   + [pltpu.VMEM((bb * t, vdim), F32)])
        valid, n_sb = t, 1
    else:
        assert bb == 1 and tb % (gsz * CHUNK) == 0
        scratch = [pltpu.VMEM((kdim, vdim), F32)]
        if fused:
            scratch += [pltpu.VMEM((tb, 4 * kdim), F32)] * 2
        valid, n_sb = CHUNK, tb // (gsz * CHUNK)
    if per_seq_state:
        og_spec = pl.BlockSpec((bb * t, vb), lambda *g: (cur(*g)[0], cur(*g)[1]))
        og_shape = (bsz * t, heads * vdim)
    else:
        og_spec = pl.BlockSpec((bb, tb, vb), lambda *g: (cur(*g)[0], cur(*g)[2], cur(*g)[1]))
        og_shape = (bsz, t, heads * vdim)

    def kern(*refs):
        if state_out is not None:
            refs = refs[:len(args) - 1] + refs[len(args):]
        _rec_kernel(*refs, mixer=mixer, fused=fused, layer_j=layer_j, has_s0=has_s0, per_seq_state=per_seq_state,
                    gsz=gsz, valid=valid, n_sb=n_sb, n_tblocks=n_tblocks)

    return pl.pallas_call(
        kern,
        grid=grid,
        in_specs=in_specs,
        out_specs=[og_spec,
                   pl.BlockSpec((None, bb, None, kdim, vdim), lambda *g: (layer_j, cur(*g)[0], cur(*g)[1], 0, 0))],
        out_shape=[jax.ShapeDtypeStruct(og_shape, BF16),
                   jax.ShapeDtypeStruct((n_layers, bsz, heads, kdim, vdim), F32)],
        scratch_shapes=scratch,
        input_output_aliases=aliases,
        compiler_params=_params(("arbitrary",) * len(grid), VMEM_LIMIT),
        name=mixer + "_rec",
    )(*args)


def _outproj_kernel(og_ref, w_ref, x_ref, g_ref, sc_ref, sh_ref, lnw_ref, lnb_ref, xo_ref, ho_ref, *, alpha):
    y = jnp.dot(og_ref[...], w_ref[...], preferred_element_type=F32)
    xn = _layer_norm(alpha * x_ref[...] + (1.0 + g_ref[...]) * y, lnw_ref[...], lnb_ref[...])
    xo_ref[...] = xn
    ho_ref[...] = (xn * (1.0 + sc_ref[...]) + sh_ref[...]).astype(BF16)


def _ln_specs(layer, sub, d, n_grid):
    idx = (lambda i: (layer * 2 + sub, 0, 0)) if n_grid == 1 else (lambda i, f: (layer * 2 + sub, 0, 0))
    return [pl.BlockSpec((None, 1, d), idx), pl.BlockSpec((None, 1, d), idx)]


def _outproj(grp, og, w_out, layer_j, x, layer, ln_w, ln_b, alpha):
    m, d = grp.m, grp.d
    sub = grp.with_tile(min(grp.tm, 256))
    tm = sub.tm
    din = w_out.shape[1]
    row = lambda i: (i, 0)
    return pl.pallas_call(
        functools.partial(_outproj_kernel, alpha=alpha),
        grid=(m // tm,),
        in_specs=[pl.BlockSpec((tm, din), row), pl.BlockSpec((None, din, d), lambda i: (layer_j, 0, 0)),
                  pl.BlockSpec((tm, d), row), sub.mod_spec(layer, 2), sub.mod_spec(layer, 4),
                  sub.mod_spec(layer, 3)] + _ln_specs(layer, 0, d, 1),
        out_specs=[pl.BlockSpec((tm, d), row), pl.BlockSpec((tm, d), row)],
        out_shape=[jax.ShapeDtypeStruct((m, d), F32), jax.ShapeDtypeStruct((m, d), BF16)],
        compiler_params=_params(("arbitrary",), VMEM_LIMIT),
        name="out_proj_ln",
    )(og, w_out, x, grp.mod, grp.mod, grp.mod, ln_w, ln_b)


def _mlp_kernel(*refs, alpha, n_f, emit_h):
    if emit_h:
        h_ref, wu_ref, wd_ref, x_ref, g_ref, sc_ref, sh_ref, lnw_ref, lnb_ref, xo_ref, ho_ref, acc_ref = refs
    else:
        h_ref, wu_ref, wd_ref, x_ref, g_ref, lnw_ref, lnb_ref, xo_ref, acc_ref = refs
    f = pl.program_id(1)

    @pl.when(f == 0)
    def _():
        acc_ref[...] = jnp.zeros(acc_ref.shape, F32)

    u = jnp.dot(h_ref[...], wu_ref[...], preferred_element_type=F32)
    u = jnp.square(jnp.maximum(u, 0.0)).astype(BF16)
    acc_ref[...] += jnp.dot(u, wd_ref[...], preferred_element_type=F32)

    @pl.when(f == n_f - 1)
    def _():
        xn = _layer_norm(alpha * x_ref[...] + (1.0 + g_ref[...]) * acc_ref[...], lnw_ref[...], lnb_ref[...])
        xo_ref[...] = xn
        if emit_h:
            ho_ref[...] = (xn * (1.0 + sc_ref[...]) + sh_ref[...]).astype(BF16)


def _mlp(grp, h, w_up, w_down, x, layer, ln_w, ln_b, alpha, emit_h):
    m, d, tm = grp.m, grp.d, grp.tm
    dff = w_up.shape[-1]
    tf = 1024
    n_f = dff // tf
    row = lambda i, f: (i, 0)
    in_specs = [pl.BlockSpec((tm, d), row), pl.BlockSpec((None, d, tf), lambda i, f: (layer, 0, f)),
                pl.BlockSpec((None, tf, d), lambda i, f: (layer, f, 0)), pl.BlockSpec((tm, d), row),
                grp.mod_spec(layer, 5)]
    args = [h, w_up, w_down, x, grp.mod]
    out_specs = [pl.BlockSpec((tm, d), row)]
    out_shape = [jax.ShapeDtypeStruct((m, d), F32)]
    if emit_h:
        in_specs += [grp.mod_spec(layer + 1, 1), grp.mod_spec(layer + 1, 0)]
        args += [grp.mod, grp.mod]
        out_specs.append(pl.BlockSpec((tm, d), row))
        out_shape.append(jax.ShapeDtypeStruct((m, d), BF16))
    in_specs += _ln_specs(layer, 1, d, 2)
    args += [ln_w, ln_b]
    outs = pl.pallas_call(
        functools.partial(_mlp_kernel, alpha=alpha, n_f=n_f, emit_h=emit_h),
        grid=(m // tm, n_f),
        in_specs=in_specs,
        out_specs=out_specs,
        out_shape=out_shape,
        scratch_shapes=[pltpu.VMEM((tm, d), F32)],
        compiler_params=_params(("arbitrary", "arbitrary"), VMEM_LIMIT),
        name="mlp_ln",
    )(*args)
    return (outs[0], outs[1]) if emit_h else (outs[0], None)


def _trunk(grp, x, s_hg, s_gla, w, rec_cfg):
    depth = w["w_up"].shape[0]
    alpha = (2.0 * depth) ** 0.25
    hg_heads, hg_k, hg_v = w["hg_dims"]
    gla_heads, gla_k, gla_v = w["gla_dims"]
    n_hg, n_gla = w["hg_w_out"].shape[0], w["gla_w_out"].shape[0]
    new_hg = new_gla = None
    h = _mod0(grp, x)
    for l in range(depth):
        j = l // 2
        if l % 2 == 0:
            if rec_cfg["hg_fused"]:
                src, w_heads = h, w["hg_w_heads"]
            else:
                src, w_heads = _inproj_heads(grp, h, w["hg_w_heads"], j), None
            og, new_hg = _recurrence("hgrn", j, n_hg, src.reshape(grp.batch, grp.seq, -1), None, w["hg_lb_raw"],
                                     w["hg_norm_w"], s_hg, new_hg, hg_heads, hg_k, hg_v,
                                     rec_cfg["hg_bb"], rec_cfg["hg_tb"], rec_cfg["hg_gsz"], w_heads)
            w_out = w["hg_w_out"]
        else:
            proj = _inproj(grp, h, w["gla_w_in"], j, w["gla_main"], rec_cfg["proj_dtype"])
            lf = _gla_gate(grp, h, w["gla_w_low"], w["gla_w_gk2"], w["gla_b_gk"], j)
            og, new_gla = _recurrence("gla", j, n_gla, proj.reshape(grp.batch, grp.seq, -1),
                                      lf.reshape(grp.batch, grp.seq, -1), None, w["gla_norm_w"], s_gla, new_gla,
                                      gla_heads, gla_k, gla_v, rec_cfg["gla_bb"], rec_cfg["tb"], rec_cfg["gla_gsz"])
            w_out = w["gla_w_out"]
        x, h2 = _outproj(grp, og.reshape(grp.m, -1), w_out, j, x, l, w["ln_w"], w["ln_b"], alpha)
        x, h = _mlp(grp, h2, w["w_up"], w["w_down"], x, l, w["ln_w"], w["ln_b"], alpha, emit_h=(l + 1 < depth))
    return x, new_hg, new_gla


def kernel(x_prompt, x_sample, state_hgrn, state_gla, c_prompt, c_sample, w_ada, b_ada, ln_w, ln_b,
           hg_w_in, hg_lb_raw, hg_norm_w, hg_w_out, gla_w_in, gla_w_gk2, gla_b_gk, gla_norm_w,
           gla_w_out, w_up, w_down):
    bp, tp, d = x_prompt.shape
    bs, ts, _ = x_sample.shape
    depth = w_up.shape[0]
    _, _, hg_heads, hg_k, hg_v = state_hgrn.shape
    _, _, gla_heads, gla_k, gla_v = state_gla.shape
    gla_main = 2 * gla_heads * gla_k + 2 * gla_heads * gla_v
    rank = gla_w_in.shape[-1] - gla_main

    ms = bs * ts
    pad_rows = (-(ms + bp)) % 8
    c_all = jnp.concatenate([jnp.repeat(c_sample, ts, axis=0), c_prompt, jnp.zeros((pad_rows, d), F32)], axis=0)
    mod = _ada(c_all, w_ada, b_ada)
    mod_p = mod[:, ms:ms + bp].reshape(depth, bp, 1, 6 * d)

    w = {
        "hg_w_heads": _regroup_heads(hg_w_in, hg_heads, 4), "hg_lb_raw": hg_lb_raw, "hg_norm_w": hg_norm_w,
        "hg_w_out": hg_w_out.astype(BF16),
        "gla_w_in": gla_w_in, "gla_main": gla_main,
        "gla_w_low": jnp.pad(gla_w_in[..., gla_main:], ((0, 0), (0, 0), (0, LANES - rank))).astype(BF16),
        "gla_w_gk2": jnp.pad(gla_w_gk2, ((0, 0), (0, LANES - rank), (0, 0))).astype(BF16),
        "gla_b_gk": gla_b_gk, "gla_norm_w": gla_norm_w, "gla_w_out": gla_w_out.astype(BF16),
        "w_up": w_up.astype(BF16), "w_down": w_down.astype(BF16),
        "ln_w": ln_w.reshape(depth * 2, 1, d), "ln_b": ln_b.reshape(depth * 2, 1, d),
        "hg_dims": (hg_heads, hg_k, hg_v), "gla_dims": (gla_heads, gla_k, gla_v),
    }

    grp_p = _Group(bp, tp, d, 512, mod_p, per_token=False)
    grp_s = _Group(bs, ts, d, 512, mod, per_token=True)

    y_p, hg_p, gla_p = _trunk(grp_p, x_prompt.reshape(bp * tp, d), None, None, w,
                              {"hg_bb": 1, "gla_bb": 1, "tb": 512, "hg_tb": 1024, "hg_gsz": 8, "gla_gsz": 8, "hg_fused": True,
                               "proj_dtype": BF16})
    y_s, hg_s, gla_s = _trunk(grp_s, x_sample.reshape(ms, d), state_hgrn, state_gla, w,
                              {"hg_bb": 32, "gla_bb": 8, "tb": ts, "hg_tb": ts, "hg_gsz": 32, "gla_gsz": 8, "hg_fused": False,
                               "proj_dtype": F32})
    return (y_p.reshape(bp, tp, d), y_s.reshape(bs, ts, d), hg_p, gla_p, hg_s, gla_s)
```
